```python
import jax
import jax.numpy as jnp
from jax import lax
import numpy as np

D_MODEL = 2048
BATCH = 4
SEQ = 4096
DEPTH = 4

GRID_W = 64
CTX_LEN = 256
N_EVEN = (DEPTH + 1) // 2
N_ODD = DEPTH // 2
EPS = 1e-6
NEG_INF = -1e30
S5_W = D_MODEL // 2
S5_CH = 16
S5_G = S5_W // S5_CH
S5_P = 64
M_W = D_MODEL // 2
M_HEADS = 4
M_DH = M_W // M_HEADS
M_CHUNK = 128
D_IN = S5_W + 4 * M_W + 4 * M_HEADS
D_MIX = S5_W + M_W
R_HEAD = 64
R_HEADS = D_MODEL // R_HEAD
R_DECAY_LORA = max(32, int(round(1.8 * D_MODEL ** 0.5 / 32)) * 32)
R_AAA_LORA = max(32, int(round(1.8 * D_MODEL ** 0.5 / 32)) * 32)
R_MV_LORA = max(32, int(round(1.3 * D_MODEL ** 0.5 / 32)) * 32)
R_GATE_LORA = max(32, int(round(0.6 * D_MODEL ** 0.8 / 32)) * 32)
R_LN_EPS = 64e-5
D_FF = ((8 * D_MODEL // 3 + 255) // 256) * 256

kernel_name = 'hybrid_s5_mlstm_rwkv7_prefix_dit'

F32 = jnp.float32


def _flip(t, rev, axis=1):
    return jnp.flip(t, axis=axis) if rev else t


def rmsnorm(x, w):
    xf = x.astype(F32)
    return xf * lax.rsqrt(jnp.mean(xf * xf, axis=-1, keepdims=True) + EPS) * w


def dwconv3(x, w, b):
    xp = jnp.pad(x, ((0, 0), (1, 1), (0, 0)))
    return xp[:, :-2] * w[0] + x * w[1] + xp[:, 2:] * w[2] + b


def qshift_grid(h):
    B, L, D = h.shape
    rows = L // GRID_W
    g = h.reshape(B, rows, GRID_W, D)
    q = D // 4
    left = jnp.pad(g[:, :, :-1, :q], ((0, 0), (0, 0), (1, 0), (0, 0)))
    right = jnp.pad(g[:, :, 1:, q:2 * q], ((0, 0), (0, 0), (0, 1), (0, 0)))
    up = jnp.pad(g[:, :-1, :, 2 * q:3 * q], ((0, 0), (1, 0), (0, 0), (0, 0)))
    down = jnp.pad(g[:, 1:, :, 3 * q:], ((0, 0), (0, 1), (0, 0), (0, 0)))
    return jnp.concatenate([left, right, up, down], axis=-1).reshape(B, L, D)


def shift_seq(h):
    half = h.shape[-1] // 2
    prev = jnp.pad(h[:, :-1, :half], ((0, 0), (1, 0), (0, 0)))
    nxt = jnp.pad(h[:, 1:, half:], ((0, 0), (0, 1), (0, 0)))
    return jnp.concatenate([prev, nxt], axis=-1)


def conv_ffn(h, w_up, conv_w, conv_b, w_down):
    u = dwconv3(h @ w_up, conv_w, conv_b)
    a, g = jnp.split(u, 2, axis=-1)
    return (a * jax.nn.silu(g)) @ w_down


def s5_discretize(lam_re, lam_im, log_step, b_re, b_im):
    step = jnp.exp(log_step.astype(F32))[:, None]
    lr, li = lam_re.astype(F32), lam_im.astype(F32)
    mag = jnp.exp(lr * step)
    ab_re, ab_im = mag * jnp.cos(li * step), mag * jnp.sin(li * step)
    den = lr * lr + li * li
    co_re = ((ab_re - 1.0) * lr + ab_im * li) / den
    co_im = (ab_im * lr - (ab_re - 1.0) * li) / den
    bb_re = co_re[..., None] * b_re - co_im[..., None] * b_im
    bb_im = co_re[..., None] * b_im + co_im[..., None] * b_re
    return ab_re, ab_im, bb_re, bb_im


def _complex_affine_combine(e1, e2):
    a1r, a1i, b1r, b1i = e1
    a2r, a2i, b2r, b2i = e2
    return (a1r * a2r - a1i * a2i, a1r * a2i + a1i * a2r,
            a2r * b1r - a2i * b1i + b2r, a2r * b1i + a2i * b1r + b2i)


def s5_scan(u, ab_re, ab_im, bb_re, bb_im, h0_re, h0_im):
    L = u.shape[1]
    bu_re = jnp.einsum('blgc,gpc->blgp', u, bb_re)
    bu_im = jnp.einsum('blgc,gpc->blgp', u, bb_im)
    bu_re = bu_re.at[:, 0].add(ab_re * h0_re - ab_im * h0_im)
    bu_im = bu_im.at[:, 0].add(ab_re * h0_im + ab_im * h0_re)
    a_re = jnp.broadcast_to(ab_re, (1, L) + ab_re.shape)
    a_im = jnp.broadcast_to(ab_im, (1, L) + ab_im.shape)
    _, _, h_re, h_im = lax.associative_scan(_complex_affine_combine, (a_re, a_im, bu_re, bu_im), axis=1)
    return h_re, h_im


def s5_readout(h_re, h_im, c_re, c_im):
    return jnp.einsum('gcp,blgp->blgc', c_re, h_re) - jnp.einsum('gcp,blgp->blgc', c_im, h_im)


def s5_output(y, u, p):
    B, L = y.shape[:2]
    y = jax.nn.gelu((y + p['d'].reshape(S5_G, S5_CH) * u).reshape(B, L, S5_W))
    return y * jax.nn.sigmoid(y @ p['w_glu'] + p['b_glu'])


def s5_mixer(u_c, u_l, p, need_ctx):
    def groups(u):
        B, L, _ = u.shape
        return u.astype(F32).reshape(B, L, S5_G, S5_CH)
    uc, ul = groups(u_c), groups(u_l)
    zero = jnp.zeros((ul.shape[0], S5_G, S5_P), F32)
    ys_c, ys_l = [], []
    for d in range(2):
        rev = d == 1
        ab_re, ab_im, bb_re, bb_im = s5_discretize(p['lam_re'][d], p['lam_im'][d], p['log_step'][d], p['b_re'][d], p['b_im'][d])
        hc_re, hc_im = s5_scan(_flip(uc, rev), ab_re, ab_im, bb_re, bb_im, zero, zero)
        hl_re, hl_im = s5_scan(_flip(ul, rev), ab_re, ab_im, bb_re, bb_im, hc_re[:, -1], hc_im[:, -1])
        ys_l.append(_flip(s5_readout(hl_re, hl_im, p['c_re'][d], p['c_im'][d]), rev))
        if need_ctx:
            ys_c.append(_flip(s5_readout(hc_re, hc_im, p['c_re'][d], p['c_im'][d]), rev))
    out_l = s5_output(ys_l[0] + ys_l[1], ul, p)
    out_c = s5_output(ys_c[0] + ys_c[1], uc, p) if need_ctx else None
    return out_c, out_l


def mlstm_chunkwise(q, k, v, i_pre, f_pre, state, emit):
    B, H, L, DH = q.shape
    nc = L // M_CHUNK

    def chunks(t):
        return jnp.moveaxis(t.reshape((B, H, nc, M_CHUNK) + t.shape[3:]), 2, 0)

    tri = jnp.tril(jnp.ones((M_CHUNK, M_CHUNK), dtype=bool))

    def step(carry, xs):
        C, n, m = carry
        qc, kc, vc, ic, lf = xs
        b = jnp.cumsum(lf, axis=-1)
        b_last = b[..., -1]
        lw_state = b_last[..., None] - b + ic
        m_new = jnp.maximum(b_last + m, lw_state.max(-1))
        e_state = jnp.exp(lw_state - m_new[..., None])
        carry_w = jnp.exp(b_last + m - m_new)
        C_new = carry_w[..., None, None] * C + jnp.einsum('bhj,bhjd,bhje->bhde', e_state, kc, vc)
        n_new = carry_w[..., None] * n + jnp.einsum('bhj,bhjd->bhd', e_state, kc)
        if not emit:
            return (C_new, n_new, m_new), None
        logw = jnp.where(tri, b[..., :, None] - b[..., None, :] + ic[..., None, :], NEG_INF)
        inter = b + m[..., None]
        m_row = jnp.maximum(logw.max(-1), inter)
        s = jnp.einsum('bhid,bhjd->bhij', qc, kc) * jnp.exp(logw - m_row[..., None])
        w_inter = jnp.exp(inter - m_row)
        num = s @ vc + w_inter[..., None] * jnp.einsum('bhid,bhde->bhie', qc, C)
        den = s.sum(-1) + w_inter * jnp.einsum('bhid,bhd->bhi', qc, n)
        h = num / jnp.maximum(jnp.abs(den), jnp.exp(-m_row))[..., None]
        return (C_new, n_new, m_new), h

    logf = jax.nn.log_sigmoid(f_pre)
    state, hs = lax.scan(step, state, (chunks(q), chunks(k), chunks(v), chunks(i_pre), chunks(logf)))
    h = jnp.moveaxis(hs, 0, 2).reshape(B, H, L, DH) if emit else None
    return h, state


def mlstm_mixer(qk_c, v_c, o_c, g_c, qk_l, v_l, o_l, g_l, p, need_ctx):
    def heads(t):
        B, L, _ = t.shape
        return t.reshape(B, L, M_HEADS, M_DH).transpose(0, 2, 1, 3).astype(F32)

    def prep(qk, v, g):
        B, L, _ = qk.shape
        qk = jax.nn.silu(dwconv3(qk, p['conv_w'], p['conv_b']))
        q = heads(qk[..., :M_W])
        k = heads(qk[..., M_W:]) * (M_DH ** -0.5)
        gates = jnp.transpose(g.astype(F32).reshape(B, L, 2, 2, M_HEADS), (2, 3, 0, 4, 1))
        return q, k, heads(v), gates

    def out(h, o):
        B, H, L, DH = h.shape
        h = h * lax.rsqrt(jnp.mean(h * h, axis=-1, keepdims=True) + EPS)
        h = h.transpose(0, 2, 1, 3).reshape(B, L, M_W) * p['norm']
        return h * jax.nn.sigmoid(o)

    qc, kc, vc, gc = prep(qk_c, v_c, g_c)
    ql, kl, vl, gl = prep(qk_l, v_l, g_l)
    B = ql.shape[0]
    st0 = (jnp.zeros((B, M_HEADS, M_DH, M_DH), F32), jnp.zeros((B, M_HEADS, M_DH), F32),
           jnp.full((B, M_HEADS), NEG_INF, F32))
    hs_c, hs_l = [], []
    for d in range(2):
        rev = d == 1
        f = lambda t: _flip(t, rev, axis=2)
        hc, st = mlstm_chunkwise(f(qc), f(kc), f(vc), f(gc[d, 0]), f(gc[d, 1]), st0, need_ctx)
        hl, _ = mlstm_chunkwise(f(ql), f(kl), f(vl), f(gl[d, 0]), f(gl[d, 1]), st, True)
        hs_l.append(f(hl))
        if need_ctx:
            hs_c.append(f(hc))
    out_l = out(hs_l[0] + hs_l[1], o_l)
    out_c = out(hs_c[0] + hs_c[1], o_c) if need_ctx else None
    return out_c, out_l


def even_mixer(h_ctx, h_lat, p, need_ctx):
    cuts = [S5_W, S5_W + 2 * M_W, S5_W + 3 * M_W, S5_W + 4 * M_W]
    u_c, qk_c, v_c, o_c, g_c = jnp.split(h_ctx @ p['w_in'] + p['b_in'], cuts, axis=-1)
    u_l, qk_l, v_l, o_l, g_l = jnp.split(h_lat @ p['w_in'] + p['b_in'], cuts, axis=-1)
    s5_c, s5_l = s5_mixer(u_c, u_l, p, need_ctx)
    ml_c, ml_l = mlstm_mixer(qk_c, v_c, o_c, g_c, qk_l, v_l, o_l, g_l, p, need_ctx)
    out_l = jnp.concatenate([s5_l, ml_l], axis=-1) @ p['w_out']
    out_c = jnp.concatenate([s5_c, ml_c], axis=-1) @ p['w_out'] if need_ctx else None
    return out_c, out_l


def rwkv7_project(h, xx, p, v_first):
    B, L, D = h.shape
    xr, xw, xk, xv, xa, xg = [h + xx * p['mu'][i] for i in range(6)]
    r = xr @ p['w_r']
    k = xk @ p['w_k']
    v = xv @ p['w_v']
    if v_first is not None:
        v = v + (v_first - v) * jax.nn.sigmoid(p['v0'] + (xv @ p['v1']) @ p['v2'])
    hd = lambda t: t.reshape(B, L, R_HEADS, R_HEAD).astype(F32)
    kk = hd(k * p['k_k'])
    kk = kk * lax.rsqrt(jnp.maximum(jnp.sum(kk * kk, axis=-1, keepdims=True), 1e-24))
    decays, ks, bs = [], [], []
    for d in range(2):
        wlog = -jax.nn.softplus(-(p['w0'][d] + jnp.tanh(xw @ p['w1'][d]) @ p['w2'][d])) - 0.5
        a = jax.nn.sigmoid(p['a0'][d] + (xa @ p['a1'][d]) @ p['a2'][d])
        decays.append(hd(jnp.exp(-jnp.exp(wlog))))
        ks.append(hd(k * (1.0 + (a - 1.0) * p['k_a'])))
        bs.append(kk * hd(a))
    return dict(r=hd(r), v=v, vh=hd(v), kk=kk, decay=decays, k=ks, b=bs, xg=xg)


def rwkv7_scan(r, w, k, v, a, b, S0, emit):
    def step(S, xs):
        rt, wt, kt, vt, at, bt = xs
        sa = jnp.einsum('bhvk,bhk->bhv', S, at)
        S = S * wt[:, :, None, :] + sa[..., None] * bt[:, :, None, :] + vt[..., None] * kt[:, :, None, :]
        return S, (jnp.einsum('bhvk,bhk->bhv', S, rt) if emit else None)
    xs = tuple(jnp.moveaxis(t, 1, 0) for t in (r, w, k, v, a, b))
    S, ys = lax.scan(step, S0, xs)
    return (jnp.moveaxis(ys, 0, 1) if emit else None), S


def rwkv7_output(y, q, p):
    B, L, H, N = y.shape
    mu = jnp.mean(y, axis=-1, keepdims=True)
    var = jnp.mean(jnp.square(y - mu), axis=-1, keepdims=True)
    y = ((y - mu) * lax.rsqrt(var + R_LN_EPS)).reshape(B, L, D_MODEL) * p['ln_w'] + p['ln_b']
    bonus = (jnp.sum(q['r'] * (q['k'][0] + q['k'][1]) * p['r_k'], axis=-1, keepdims=True) * q['vh']).reshape(B, L, D_MODEL)
    g = jax.nn.sigmoid(q['xg'] @ p['g1']) @ p['g2']
    return ((y + bonus) * g) @ p['w_o']


def rwkv7_mixer(h_c, h_l, p, vf_c, vf_l, need_ctx):
    pc = rwkv7_project(h_c, shift_seq(h_c) - h_c, p, vf_c)
    pl = rwkv7_project(h_l, qshift_grid(h_l) - h_l, p, vf_l)
    S0 = jnp.zeros((h_l.shape[0], R_HEADS, R_HEAD, R_HEAD), F32)
    ys_c, ys_l = [], []
    for d in range(2):
        rev = d == 1
        args_c = [_flip(t, rev) for t in (pc['r'], pc['decay'][d], pc['k'][d], pc['vh'], -pc['kk'], pc['b'][d])]
        yc, S = rwkv7_scan(*args_c, S0, need_ctx)
        args_l = [_flip(t, rev) for t in (pl['r'], pl['decay'][d], pl['k'][d], pl['vh'], -pl['kk'], pl['b'][d])]
        yl, _ = rwkv7_scan(*args_l, S, True)
        ys_l.append(_flip(yl, rev))
        if need_ctx:
            ys_c.append(_flip(yc, rev))
    out_l = rwkv7_output(ys_l[0] + ys_l[1], pl, p)
    out_c = rwkv7_output(ys_c[0] + ys_c[1], pc, p) if need_ctx else None
    return out_c, out_l, pc['v'], pl['v']


def setup_inputs(seed: int = 0) -> dict:
    key = jax.random.key(seed)
    ks = iter(jax.random.split(key, 64))

    def nrm(shape, scale):
        return jax.random.normal(next(ks), shape, F32) * scale

    def unif(shape, lo, hi):
        return jax.random.uniform(next(ks), shape, F32, lo, hi)

    D = D_MODEL
    g0 = S5_W + 4 * M_W
    fbias = jnp.linspace(3.0, 6.0, M_HEADS, dtype=F32)
    b_in = nrm((N_EVEN, D_IN), 0.02)
    b_in = b_in.at[:, g0 + M_HEADS:g0 + 2 * M_HEADS].add(fbias).at[:, g0 + 3 * M_HEADS:].add(fbias)
    n_idx = jnp.arange(S5_P, dtype=F32)
    decay_profile = -6.5 + 5.0 * (jnp.arange(D, dtype=F32) / (D - 1)) ** 0.9
    return {
        'x': nrm((BATCH, SEQ, D), 1.0),
        'c': nrm((BATCH, D), 1.0),
        'ctx': nrm((BATCH, CTX_LEN, D), 1.0),
        'c_ctx': nrm((D,), 1.0),
        'ada_w': nrm((DEPTH, D, 6 * D), 0.5 * D ** -0.5),
        'ada_b': nrm((DEPTH, 6 * D), 0.02),
        'norm_mix': 1.0 + nrm((DEPTH, D), 0.02),
        'norm_ffn': 1.0 + nrm((DEPTH, D), 0.02),
        'ffn_w_up': nrm((DEPTH, D, 2 * D_FF), D ** -0.5),
        'ffn_conv_w': nrm((DEPTH, 3, 2 * D_FF), 3 ** -0.5),
        'ffn_conv_b': nrm((DEPTH, 2 * D_FF), 0.02),
        'ffn_w_down': nrm((DEPTH, D_FF, D), D_FF ** -0.5),
        'norm_final': 1.0 + nrm((D,), 0.02),
        'ev_w_in': nrm((N_EVEN, D, D_IN), D ** -0.5),
        'ev_b_in': b_in,
        'ev_w_out': nrm((N_EVEN, D_MIX, D), D_MIX ** -0.5),
        's5_lam_re': -0.5 + nrm((N_EVEN, 2, S5_G, S5_P), 0.01),
        's5_lam_im': jnp.pi * n_idx + nrm((N_EVEN, 2, S5_G, S5_P), 0.01),
        's5_log_step': unif((N_EVEN, 2, S5_G), float(np.log(1e-3)), float(np.log(1e-1))),
        's5_b_re': nrm((N_EVEN, 2, S5_G, S5_P, S5_CH), (2 * S5_CH) ** -0.5),
        's5_b_im': nrm((N_EVEN, 2, S5_G, S5_P, S5_CH), (2 * S5_CH) ** -0.5),
        's5_c_re': nrm((N_EVEN, 2, S5_G, S5_CH, S5_P), S5_P ** -0.5),
        's5_c_im': nrm((N_EVEN, 2, S5_G, S5_CH, S5_P), S5_P ** -0.5),
        's5_d': nrm((N_EVEN, S5_W), 1.0),
        's5_w_glu': nrm((N_EVEN, S5_W, S5_W), S5_W ** -0.5),
        's5_b_glu': nrm((N_EVEN, S5_W), 0.02),
        'ml_conv_w': nrm((N_EVEN, 3, 2 * M_W), 3 ** -0.5),
        'ml_conv_b': nrm((N_EVEN, 2 * M_W), 0.02),
        'ml_norm': 1.0 + nrm((N_EVEN, M_W), 0.02),
        'rw_mu': unif((N_ODD, 6, D), 0.0, 1.0),
        'rw_w_r': nrm((N_ODD, D, D), D ** -0.5),
        'rw_w_k': nrm((N_ODD, D, D), D ** -0.5),
        'rw_w_v': nrm((N_ODD, D, D), D ** -0.5),
        'rw_w_o': nrm((N_ODD, D, D), D ** -0.5),
        'rw_w0': decay_profile + nrm((N_ODD, 2, D), 0.1),
        'rw_w1': nrm((N_ODD, 2, D, R_DECAY_LORA), D ** -0.5),
        'rw_w2': nrm((N_ODD, 2, R_DECAY_LORA, D), 0.1 * R_DECAY_LORA ** -0.5),
        'rw_a0': nrm((N_ODD, 2, D), 0.1),
        'rw_a1': nrm((N_ODD, 2, D, R_AAA_LORA), D ** -0.5),
        'rw_a2': nrm((N_ODD, 2, R_AAA_LORA, D), 0.1 * R_AAA_LORA ** -0.5),
        'rw_v0': 1.0 + nrm((N_ODD - 1, D), 0.1),
        'rw_v1': nrm((N_ODD - 1, D, R_MV_LORA), D ** -0.5),
        'rw_v2': nrm((N_ODD - 1, R_MV_LORA, D), 0.1 * R_MV_LORA ** -0.5),
        'rw_g1': nrm((N_ODD, D, R_GATE_LORA), D ** -0.5),
        'rw_g2': nrm((N_ODD, R_GATE_LORA, D), R_GATE_LORA ** -0.5),
        'rw_k_k': 0.85 + nrm((N_ODD, D), 0.02),
        'rw_k_a': 1.0 + nrm((N_ODD, D), 0.02),
        'rw_r_k': nrm((N_ODD, R_HEADS, R_HEAD), 0.1),
        'rw_ln_w': 1.0 + nrm((N_ODD, D), 0.02),
        'rw_ln_b': nrm((N_ODD, D), 0.02),
    }


def reference(x, c, ctx, c_ctx, ada_w, ada_b, norm_mix, norm_ffn, ffn_w_up, ffn_conv_w, ffn_conv_b, ffn_w_down,
              norm_final, ev_w_in, ev_b_in, ev_w_out, s5_lam_re, s5_lam_im, s5_log_step, s5_b_re, s5_b_im,
              s5_c_re, s5_c_im, s5_d, s5_w_glu, s5_b_glu, ml_conv_w, ml_conv_b, ml_norm, rw_mu, rw_w_r, rw_w_k,
              rw_w_v, rw_w_o, rw_w0, rw_w1, rw_w2, rw_a0, rw_a1, rw_a2, rw_v0, rw_v1, rw_v2, rw_g1, rw_g2,
              rw_k_k, rw_k_a, rw_r_k, rw_ln_w, rw_ln_b):
    v_first_ctx = None
    v_first_lat = None
    for l in range(DEPTH):
        need_ctx = l < DEPTH - 1
        j = l // 2
        mod_lat = (jax.nn.silu(c) @ ada_w[l] + ada_b[l])[:, None, :]
        mod_ctx = jax.nn.silu(c_ctx) @ ada_w[l] + ada_b[l]
        sh1, sc1, g1, sh2, sc2, g2 = jnp.split(mod_lat, 6, axis=-1)
        csh1, csc1, cg1, csh2, csc2, cg2 = jnp.split(mod_ctx, 6, axis=-1)
        h_lat = rmsnorm(x, norm_mix[l]) * (1.0 + sc1) + sh1
        h_ctx = rmsnorm(ctx, norm_mix[l]) * (1.0 + csc1) + csh1
        if l % 2 == 0:
            p = dict(w_in=ev_w_in[j], b_in=ev_b_in[j], w_out=ev_w_out[j], lam_re=s5_lam_re[j], lam_im=s5_lam_im[j],
                     log_step=s5_log_step[j], b_re=s5_b_re[j], b_im=s5_b_im[j], c_re=s5_c_re[j], c_im=s5_c_im[j],
                     d=s5_d[j], w_glu=s5_w_glu[j], b_glu=s5_b_glu[j], conv_w=ml_conv_w[j], conv_b=ml_conv_b[j],
                     norm=ml_norm[j])
            o_ctx, o_lat = even_mixer(h_ctx, h_lat, p, need_ctx)
        else:
            p = dict(mu=rw_mu[j], w_r=rw_w_r[j], w_k=rw_w_k[j], w_v=rw_w_v[j], w_o=rw_w_o[j], w0=rw_w0[j],
                     w1=rw_w1[j], w2=rw_w2[j], a0=rw_a0[j], a1=rw_a1[j], a2=rw_a2[j], g1=rw_g1[j], g2=rw_g2[j],
                     k_k=rw_k_k[j], k_a=rw_k_a[j], r_k=rw_r_k[j], ln_w=rw_ln_w[j], ln_b=rw_ln_b[j])
            if j > 0:
                p.update(v0=rw_v0[j - 1], v1=rw_v1[j - 1], v2=rw_v2[j - 1])
            o_ctx, o_lat, v_ctx, v_lat = rwkv7_mixer(h_ctx, h_lat, p, v_first_ctx, v_first_lat, need_ctx)
            if j == 0:
                v_first_ctx, v_first_lat = v_ctx, v_lat
        x = x + g1 * o_lat
        x = x + g2 * conv_ffn(rmsnorm(x, norm_ffn[l]) * (1.0 + sc2) + sh2,
                              ffn_w_up[l], ffn_conv_w[l], ffn_conv_b[l], ffn_w_down[l])
        if need_ctx:
            ctx = ctx + cg1 * o_ctx
            ctx = ctx + cg2 * conv_ffn(rmsnorm(ctx, norm_ffn[l]) * (1.0 + csc2) + csh2,
                                       ffn_w_up[l], ffn_conv_w[l], ffn_conv_b[l], ffn_w_down[l])
    return rmsnorm(x, norm_final)
```

```python
import functools

import jax
import jax.numpy as jnp
from jax import lax
from jax.experimental import pallas as pl
from jax.experimental.pallas import tpu as pltpu

F32 = jnp.float32
BF16 = jnp.bfloat16
EPS = 1e-6
NEG_INF = -1e30
GRID_W = 64
S5_T = 16
M_HEADS = 4
M_CHUNK = 128
R_HEAD = 64
R_CHUNK = 64
R_LN_EPS = 64e-5
HIGHEST = lax.Precision.HIGHEST
VMEM_LIMIT = 56 * 1024 * 1024


def _cparams(sem):
    return pltpu.CompilerParams(dimension_semantics=sem, vmem_limit_bytes=VMEM_LIMIT)


def _dot(a, b):
    return jnp.dot(a, b, preferred_element_type=F32)


def _dot_nt(a, b):
    return lax.dot_general(a, b, (((1,), (1,)), ((), ())), preferred_element_type=F32)


def _dot_tn(a, b):
    return lax.dot_general(a, b, (((0,), (0,)), ((), ())), preferred_element_type=F32)


def _split3(x):
    h = x.astype(BF16)
    r = x - h.astype(F32)
    m = r.astype(BF16)
    l = (r - m.astype(F32)).astype(BF16)
    return h, m, l


def _dot_exact_lhs(a_bf16, x):
    h, m, l = _split3(x)
    return _dot(a_bf16, h) + _dot(a_bf16, m) + _dot(a_bf16, l)


def _dot_exact_rhs(x, b_bf16):
    h, m, l = _split3(x)
    return _dot(h, b_bf16) + _dot(m, b_bf16) + _dot(l, b_bf16)


def _sigmoid(x):
    return jax.nn.sigmoid(x)


def _silu(x):
    return x * jax.nn.sigmoid(x)


def _gelu_tanh(x):
    return 0.5 * x * (1.0 + jnp.tanh(0.7978845608028654 * (x + 0.044715 * (x * x * x))))


def _softplus(x):
    return jnp.maximum(x, 0.0) + jnp.log(1.0 + jnp.exp(-jnp.abs(x)))


def _row_iota(n):
    return lax.broadcasted_iota(jnp.int32, (n, 1), 0)


def _mod_rows(ref, b, is_ctx):
    lat = ref[pl.ds(b, 1), :]
    ctx = ref[pl.ds(_CTX_ROW, 1), :]
    return jnp.where(is_ctx, ctx, lat)


_CTX_ROW = 7


def _norm_mod(x, nw, sc, sh):
    ms = jnp.mean(x * x, axis=-1, keepdims=True)
    return x * lax.rsqrt(ms + EPS) * nw * (1.0 + sc) + sh


def _adaln_kernel(c_ref, w_ref, b_ref, o_ref):
    a = _silu(c_ref[...])
    o_ref[0] = _dot(a.astype(BF16), w_ref[0].astype(BF16)) + b_ref[0]


def adaln(c8, ada_w, ada_b, tn=1024):
    depth, d, n6 = ada_w.shape
    return pl.pallas_call(
        _adaln_kernel,
        out_shape=jax.ShapeDtypeStruct((depth, 8, n6), F32),
        grid=(depth, n6 // tn),
        in_specs=[pl.BlockSpec((8, d), lambda l, j: (0, 0)),
                  pl.BlockSpec((1, d, tn), lambda l, j: (l, 0, j)),
                  pl.BlockSpec((1, 1, tn), lambda l, j: (l, 0, j))],
        out_specs=pl.BlockSpec((1, 8, tn), lambda l, j: (l, 0, j)),
        compiler_params=_cparams(("parallel", "parallel")),
        name="adaln",
    )(c8, ada_w, ada_b.reshape(depth, 1, n6))


def _halo_specs(tm, d, n_rows, halo):
    r = tm // halo
    nblk = n_rows // halo
    return [pl.BlockSpec((tm, d), lambda i, j: (i, 0)),
            pl.BlockSpec((halo, d), lambda i, j: (jnp.maximum(i * r - 1, 0), 0)),
            pl.BlockSpec((halo, d), lambda i, j: (jnp.minimum(i * r + r, nblk - 1), 0))]


def _mod_spec(d, k):
    return pl.BlockSpec((8, d), lambda i, j: (0, k))


def _conv3(u, first, last, cw, cb):
    n = u.shape[0]
    up = jnp.where(first, 0.0, pltpu.roll(u, 1, 0))
    un = jnp.where(last, 0.0, pltpu.roll(u, n - 1, 0))
    return up * cw[0:1] + u * cw[1:2] + un * cw[2:3] + cb


def _ext_rows(xm_ref, xp_ref, xn_ref, sh_ref, sc_ref, nw_ref, i, tm, tpb, ctx, halo):
    b = i // tpb
    pos = (i % tpb) * tm - halo + _row_iota(tm + 2 * halo)
    is_ctx = pos < ctx
    xe = jnp.concatenate([xp_ref[...], xm_ref[...], xn_ref[...]], axis=0)
    h = _norm_mod(xe, nw_ref[...], _mod_rows(sc_ref, b, is_ctx), _mod_rows(sh_ref, b, is_ctx))
    return h, pos


def _ffn_kernel(xm_ref, xp_ref, xn_ref, sh_ref, sc_ref, g_ref, nw_ref, wa_ref, wg_ref,
                cwa_ref, cwg_ref, cba_ref, cbg_ref, wd_ref, o_ref, h_scr, acc_scr,
                *, tm, tpb, ctx, lc):
    i = pl.program_id(0)
    j = pl.program_id(1)

    @pl.when(j == 0)
    def _():
        h, _ = _ext_rows(xm_ref, xp_ref, xn_ref, sh_ref, sc_ref, nw_ref, i, tm, tpb, ctx, 8)
        h_scr[...] = h.astype(BF16)
        acc_scr[...] = jnp.zeros_like(acc_scr)

    pos = (i % tpb) * tm - 8 + _row_iota(tm + 16)
    first = (pos == 0) | (pos == ctx)
    last = (pos == ctx - 1) | (pos == lc - 1)
    h = h_scr[...]
    a = _conv3(_dot(h, wa_ref[...]), first, last, cwa_ref[...], cba_ref[...])[8:8 + tm]
    g = _conv3(_dot(h, wg_ref[...]), first, last, cwg_ref[...], cbg_ref[...])[8:8 + tm]
    act = (a * _silu(g)).astype(BF16)
    acc_scr[...] += _dot(act, wd_ref[...])

    @pl.when(j == pl.num_programs(1) - 1)
    def _():
        b = i // tpb
        is_ctx = ((i % tpb) * tm + _row_iota(tm)) < ctx
        o_ref[...] = xm_ref[...] + _mod_rows(g_ref, b, is_ctx) * acc_scr[...]


def conv_ffn_block(x, mod, nw, w_up, conv_w, conv_b, w_down, *, lc, ctx, tm, tn=256):
    n, d = x.shape
    dff = w_down.shape[0]
    nj = dff // tn
    kern = functools.partial(_ffn_kernel, tm=tm, tpb=lc // tm, ctx=ctx, lc=lc)
    return pl.pallas_call(
        kern,
        out_shape=jax.ShapeDtypeStruct((n, d), F32),
        grid=(n // tm, nj),
        in_specs=_halo_specs(tm, d, n, 8) + [
            _mod_spec(d, 3), _mod_spec(d, 4), _mod_spec(d, 5),
            pl.BlockSpec((1, d), lambda i, j: (0, 0)),
            pl.BlockSpec((d, tn), lambda i, j: (0, j)),
            pl.BlockSpec((d, tn), lambda i, j: (0, j + nj)),
            pl.BlockSpec((3, tn), lambda i, j: (0, j)),
            pl.BlockSpec((3, tn), lambda i, j: (0, j + nj)),
            pl.BlockSpec((1, tn), lambda i, j: (0, j)),
            pl.BlockSpec((1, tn), lambda i, j: (0, j + nj)),
            pl.BlockSpec((tn, d), lambda i, j: (j, 0)),
        ],
        out_specs=pl.BlockSpec((tm, d), lambda i, j: (i, 0)),
        scratch_shapes=[pltpu.VMEM((tm + 16, d), BF16), pltpu.VMEM((tm, d), F32)],
        compiler_params=_cparams(("parallel", "arbitrary")),
        name="conv_ffn",
    )(x, x, x, mod, mod, mod, nw.reshape(1, d), w_up, w_up, conv_w, conv_w,
      conv_b.reshape(1, -1), conv_b.reshape(1, -1), w_down)


def _final_norm_kernel(x_ref, w_ref, o_ref):
    x = x_ref[...]
    o_ref[...] = x * lax.rsqrt(jnp.mean(x * x, axis=-1, keepdims=True) + EPS) * w_ref[...]


def final_norm(x, w, *, batch, lc, ctx, tm=256):
    n, d = x.shape
    tpb = lc // tm
    skip = ctx // tm
    per = tpb - skip
    return pl.pallas_call(
        _final_norm_kernel,
        out_shape=jax.ShapeDtypeStruct((batch * per * tm, d), F32),
        grid=(batch, per),
        in_specs=[pl.BlockSpec((tm, d), lambda b, t: (b * tpb + skip + t, 0)),
                  pl.BlockSpec((1, d), lambda b, t: (0, 0))],
        out_specs=pl.BlockSpec((tm, d), lambda b, t: (b * per + t, 0)),
        compiler_params=_cparams(("parallel", "parallel")),
        name="final_norm",
    )(x, w.reshape(1, d))


def _normproj_kernel(*refs, tm, tpb, ctx, lc, conv):
    if conv:
        (xm_ref, xp_ref, xn_ref, sh_ref, sc_ref, nw_ref, w_ref, b_ref, cw_ref, cb_ref, s_ref,
         o_ref, h_scr) = refs
    else:
        xm_ref, sh_ref, sc_ref, nw_ref, w_ref, b_ref, o_ref, h_scr = refs
    i = pl.program_id(0)
    j = pl.program_id(1)

    @pl.when(j == 0)
    def _():
        if conv:
            h, _ = _ext_rows(xm_ref, xp_ref, xn_ref, sh_ref, sc_ref, nw_ref, i, tm, tpb, ctx, 8)
        else:
            b = i // tpb
            is_ctx = ((i % tpb) * tm + _row_iota(tm)) < ctx
            h = _norm_mod(xm_ref[...], nw_ref[...], _mod_rows(sc_ref, b, is_ctx),
                          _mod_rows(sh_ref, b, is_ctx))
        h_scr[...] = h.astype(BF16)

    u = _dot(h_scr[...], w_ref[...]) + b_ref[...]
    if conv:
        pos = (i % tpb) * tm - 8 + _row_iota(tm + 16)
        first = (pos == 0) | (pos == ctx)
        last = (pos == ctx - 1) | (pos == lc - 1)
        u = _silu(_conv3(u, first, last, cw_ref[...], cb_ref[...])[8:8 + tm]) * s_ref[...]
    o_ref[...] = u.astype(o_ref.dtype)


def norm_proj(x, mod, nw, w, bias, *, lc, ctx, tm, tn, out_dtype, conv=None):
    n, d = x.shape
    ncol = w.shape[1]
    kern = functools.partial(_normproj_kernel, tm=tm, tpb=lc // tm, ctx=ctx, lc=lc,
                             conv=conv is not None)
    col = lambda r: pl.BlockSpec((r, tn), lambda i, j: (0, j))
    if conv is not None:
        cw, cb, scale = conv
        in_specs = _halo_specs(tm, d, n, 8) + [
            _mod_spec(d, 0), _mod_spec(d, 1), pl.BlockSpec((1, d), lambda i, j: (0, 0)),
            pl.BlockSpec((d, tn), lambda i, j: (0, j)), col(1), col(3), col(1), col(1)]
        args = (x, x, x, mod, mod, nw.reshape(1, d), w, bias.reshape(1, ncol), cw,
                cb.reshape(1, ncol), scale.reshape(1, ncol))
        rows = tm + 16
    else:
        in_specs = [pl.BlockSpec((tm, d), lambda i, j: (i, 0)),
                    _mod_spec(d, 0), _mod_spec(d, 1), pl.BlockSpec((1, d), lambda i, j: (0, 0)),
                    pl.BlockSpec((d, tn), lambda i, j: (0, j)), col(1)]
        args = (x, mod, mod, nw.reshape(1, d), w, bias.reshape(1, ncol))
        rows = tm
    return pl.pallas_call(
        kern,
        out_shape=jax.ShapeDtypeStruct((n, ncol), out_dtype),
        grid=(n // tm, ncol // tn),
        in_specs=in_specs,
        out_specs=pl.BlockSpec((tm, tn), lambda i, j: (i, j)),
        scratch_shapes=[pltpu.VMEM((rows, d), BF16)],
        compiler_params=_cparams(("parallel", "arbitrary")),
        name="norm_proj_conv" if conv is not None else "norm_proj",
    )(*args)


def _s5_param_kernel(lr_ref, li_ref, ls_ref, bbr_ref, bbi_ref, cr_ref, ci_ref,
                     wre_ref, wim_ref, ere_ref, eim_ref, k_ref, lam_re_ref, lam_im_ref, *, t, ch):
    step = jnp.exp(ls_ref[0])
    lr = lr_ref[0]
    li = li_ref[0]
    p = lr.shape[-1]
    sr = lr * step
    si = li * step

    def power(tau):
        mag = jnp.exp(tau * sr)
        return mag * jnp.cos(tau * si), mag * jnp.sin(tau * si)

    ab_re, ab_im = power(1.0)
    den = lr * lr + li * li
    co_re = ((ab_re - 1.0) * lr + ab_im * li) / den
    co_im = (ab_im * lr - (ab_re - 1.0) * li) / den
    b_re = bbr_ref[0]
    b_im = bbi_ref[0]
    bb_re = co_re * b_re - co_im * b_im
    bb_im = co_re * b_im + co_im * b_re
    tile = lambda m: jnp.concatenate([m] * t, axis=0)
    tau = (lax.broadcasted_iota(jnp.int32, (t * ch, p), 0) // ch).astype(F32)
    pr, pi = power(tau)
    bbr_t, bbi_t = tile(bb_re), tile(bb_im)
    w_re = pr * bbr_t - pi * bbi_t
    w_im = pr * bbi_t + pi * bbr_t
    wre_ref[0] = w_re
    wim_ref[0] = w_im
    qr, qi = power(tau + 1.0)
    c_re = cr_ref[0]
    c_im = ci_ref[0]
    cr_t, ci_t = tile(c_re), tile(c_im)
    ere_ref[0] = cr_t * qr - ci_t * qi
    eim_ref[0] = -(cr_t * qi + ci_t * qr)
    k_ref[0] = (lax.dot_general(c_re, w_re, (((1,), (1,)), ((), ())), precision=HIGHEST,
                                preferred_element_type=F32)
                - lax.dot_general(c_im, w_im, (((1,), (1,)), ((), ())), precision=HIGHEST,
                                  preferred_element_type=F32))
    lt_re, lt_im = power(float(t))
    lam_re_ref[0] = lt_re
    lam_im_ref[0] = lt_im


def s5_params(lam_re, lam_im, log_step, b_re, b_im, c_re, c_im, t=S5_T):
    nd, g, p = lam_re.shape
    ch = c_re.shape[2]
    m = nd * g
    r3 = lambda a: a.reshape(m, 1, p)
    bt = lambda a: jnp.swapaxes(a, -1, -2).reshape(m, ch, p)
    vec = pl.BlockSpec((1, 1, p), lambda i: (i, 0, 0))
    mat = pl.BlockSpec((1, ch, p), lambda i: (i, 0, 0))
    big = pl.BlockSpec((1, t * ch, p), lambda i: (i, 0, 0))
    outs = pl.pallas_call(
        functools.partial(_s5_param_kernel, t=t, ch=ch),
        out_shape=[jax.ShapeDtypeStruct((m, t * ch, p), F32)] * 4
        + [jax.ShapeDtypeStruct((m, ch, t * ch), F32)] + [jax.ShapeDtypeStruct((m, 1, p), F32)] * 2,
        grid=(m,),
        in_specs=[vec, vec, pl.BlockSpec((1, 1, 1), lambda i: (i, 0, 0)), mat, mat, mat, mat],
        out_specs=[big] * 4 + [pl.BlockSpec((1, ch, t * ch), lambda i: (i, 0, 0)), vec, vec],
        compiler_params=_cparams(("parallel",)),
        name="s5_params",
    )(r3(lam_re), r3(lam_im), log_step.reshape(m, 1, 1), bt(b_re), bt(b_im),
      c_re.reshape(m, ch, p), c_im.reshape(m, ch, p))
    w_re, w_im, e_re, e_im, k, l_re, l_im = outs
    sh = lambda a: a.reshape(nd, g, t, ch, p)
    return (sh(w_re), sh(w_im), sh(e_re), sh(e_im), k.reshape(nd, g, ch, t, ch),
            l_re.reshape(nd, g, p), l_im.reshape(nd, g, p))


def s5_assemble(params, d_skip, t=S5_T):
    w_re, w_im, e_re, e_im, k, l_re, l_im = params
    nd, g, _, ch, p = w_re.shape
    tc = t * ch
    s_i = jnp.arange(t)[:, None]
    t_i = jnp.arange(t)[None, :]
    kf = jnp.take(k[0], jnp.clip(t_i - s_i, 0, t - 1), axis=2)
    kb = jnp.take(k[1], jnp.clip(s_i - t_i, 0, t - 1), axis=2)
    mf = jnp.where((t_i >= s_i)[None, None, :, :, None], kf, 0.0)
    mb = jnp.where((s_i >= t_i)[None, None, :, :, None], kb, 0.0)
    m = jnp.transpose(mf + mb, (0, 2, 4, 3, 1))
    eye = (jnp.eye(t)[:, None, :, None] * jnp.eye(ch)[None, :, None, :])
    m = m + eye[None] * d_skip.reshape(g, 1, 1, 1, ch)
    m = m.reshape(g, tc, tc)
    def fmat(wr, wi):
        wr = wr.reshape(g, tc, p)
        wi = wi.reshape(g, tc, p)
        return jnp.concatenate([wr, wi, wi, wr], axis=-1)
    f2 = jnp.concatenate([fmat(w_re[0][:, ::-1], w_im[0][:, ::-1]), fmat(w_re[1], w_im[1])], axis=-1)
    def emat(er, ei):
        return jnp.concatenate([jnp.swapaxes(er.reshape(g, tc, p), 1, 2),
                                jnp.swapaxes(ei.reshape(g, tc, p), 1, 2)], axis=1)
    e2 = jnp.concatenate([emat(e_re[0], e_im[0]), emat(e_re[1][:, ::-1], e_im[1][:, ::-1])], axis=1)
    def coef(lr, li):
        a = jnp.concatenate([lr, lr], axis=-1)
        b1 = jnp.concatenate([-li, li], axis=-1)
        b2 = jnp.concatenate([li, -li], axis=-1)
        return jnp.concatenate([a, a], axis=-1), jnp.concatenate([b1, b2], axis=-1)
    af, bf = coef(l_re[0], l_im[0])
    ab, bb = coef(l_re[1], l_im[1])
    coefs = jnp.stack([jnp.concatenate([af, ab], axis=-1), jnp.concatenate([bf, bb], axis=-1)], axis=1)
    return m.astype(BF16), f2.astype(BF16), e2.astype(BF16), coefs


def _s5_in_kernel(u_ref, f_ref, z_ref):
    z_ref[...] = _dot(u_ref[0], f_ref[0])


def s5_chunk_inputs(ut, f2):
    g, r, tc = ut.shape
    w = f2.shape[-1]
    return pl.pallas_call(
        _s5_in_kernel,
        out_shape=jax.ShapeDtypeStruct((r, g * w), F32),
        grid=(g,),
        in_specs=[pl.BlockSpec((1, r, tc), lambda i: (i, 0, 0)),
                  pl.BlockSpec((1, tc, w), lambda i: (i, 0, 0))],
        out_specs=pl.BlockSpec((r, w), lambda i: (0, i)),
        compiler_params=_cparams(("parallel",)),
        name="s5_chunk_inputs",
    )(ut, f2)


def _s5_scan_kernel(z_ref, c_ref, h_ref, *, gb, nc, nc_ctx, lw):
    nb = z_ref.shape[0]
    ng, ng_ctx = nc // 8, nc_ctx // 8

    def body(jg, carry):
        rows = (pl.multiple_of(jg * 8, 8),
                pl.multiple_of(_chunk_order(1, jg, ng, ng_ctx) * 8, 8))
        new = list(carry)
        for gi in range(gb):
            for di in range(2):
                idx = 2 * (2 * gi + di)
                hs, hx = new[idx], new[idx + 1]
                base = (gi * 4 + 2 * di) * lw
                z8 = z_ref[:, pl.ds(rows[di], 8), base:base + lw]
                zx8 = z_ref[:, pl.ds(rows[di], 8), base + lw:base + 2 * lw]
                a = c_ref[gi, 0:1, 2 * di * lw:(2 * di + 1) * lw]
                b1 = c_ref[gi, 1:2, 2 * di * lw:(2 * di + 1) * lw]
                b2 = c_ref[gi, 1:2, (2 * di + 1) * lw:(2 * di + 2) * lw]
                entry = [None] * 8
                for s in range(8):
                    r = s if di == 0 else 7 - s
                    entry[r] = hs
                    hs, hx = (a * hs + b1 * hx + z8[:, r:r + 1, :],
                              a * hx + b2 * hs + zx8[:, r:r + 1, :])
                new[idx], new[idx + 1] = hs, hx
                h_ref[:, pl.ds(rows[di], 8), (gi * 2 + di) * lw:(gi * 2 + di + 1) * lw] = (
                    jnp.concatenate(entry, axis=1))
        return tuple(new)

    init = tuple(jnp.zeros((nb, 1, lw), F32) for _ in range(4 * gb))
    lax.fori_loop(0, ng, body, init)


def s5_chunk_scan(z, coefs, *, batch, nc, nc_ctx, gb=2):
    r, wtot = z.shape
    g = coefs.shape[0]
    lw = wtot // g // 4
    assert nc % 8 == 0 and nc_ctx % 8 == 0
    z3 = z.reshape(batch, nc, wtot)
    return pl.pallas_call(
        functools.partial(_s5_scan_kernel, gb=gb, nc=nc, nc_ctx=nc_ctx, lw=lw),
        out_shape=jax.ShapeDtypeStruct((batch, nc, g * 2 * lw), F32),
        grid=(g // gb,),
        in_specs=[pl.BlockSpec((batch, nc, gb * 4 * lw), lambda i: (0, 0, i)),
                  pl.BlockSpec((gb, 2, 4 * lw), lambda i: (i, 0, 0))],
        out_specs=pl.BlockSpec((batch, nc, gb * 2 * lw), lambda i: (0, 0, i)),
        compiler_params=_cparams(("parallel",)),
        name="s5_chunk_scan",
    )(z3, coefs).reshape(r, g * 2 * lw)


def _s5_out_kernel(u_ref, m_ref, h_ref, e_ref, y_ref):
    y = _dot(u_ref[0], m_ref[0]) + _dot(h_ref[...].astype(BF16), e_ref[0])
    y_ref[0] = y.astype(y_ref.dtype)


def s5_chunk_outputs(ut, m, hs, e2):
    g, r, tc = ut.shape
    hw = e2.shape[1]
    return pl.pallas_call(
        _s5_out_kernel,
        out_shape=jax.ShapeDtypeStruct((g, r, tc), BF16),
        grid=(g,),
        in_specs=[pl.BlockSpec((1, r, tc), lambda i: (i, 0, 0)),
                  pl.BlockSpec((1, tc, tc), lambda i: (i, 0, 0)),
                  pl.BlockSpec((r, hw), lambda i: (0, i)),
                  pl.BlockSpec((1, hw, tc), lambda i: (i, 0, 0))],
        out_specs=pl.BlockSpec((1, r, tc), lambda i: (i, 0, 0)),
        compiler_params=_cparams(("parallel",)),
        name="s5_chunk_outputs",
    )(ut, m, hs, e2)


def s5_mix(u, ops, *, batch, lc, ctx, t=S5_T):
    m, f2, e2, coefs = ops
    n, w = u.shape
    g = m.shape[0]
    ch = w // g
    r = n // t
    ut = jnp.transpose(u.reshape(r, t, g, ch), (2, 0, 1, 3)).reshape(g, r, t * ch)
    z = s5_chunk_inputs(ut, f2)
    hs = s5_chunk_scan(z, coefs, batch=batch, nc=lc // t, nc_ctx=ctx // t)
    yt = s5_chunk_outputs(ut, m, hs, e2)
    return jnp.transpose(yt.reshape(g, r, t, ch), (1, 2, 0, 3)).reshape(n, w)


def _chunk_order(d, j, nc, nc_ctx):
    bwd = jnp.where(j < nc_ctx, nc_ctx - 1 - j, nc - 1 - (j - nc_ctx))
    return jnp.where(d == 0, j, bwd)


def _mlstm_kernel(q_ref, k_ref, v_ref, g_ref, o_ref, c_scr, n_scr, m_scr, *, nh):
    hd = pl.program_id(1)
    d = pl.program_id(2)
    j = pl.program_id(3)
    tc = q_ref.shape[0]

    @pl.when(j == 0)
    def _():
        c_scr[...] = jnp.zeros_like(c_scr)
        n_scr[...] = jnp.zeros_like(n_scr)
        m_scr[...] = jnp.full_like(m_scr, NEG_INF)

    gates = g_ref[...]
    lane = lax.broadcasted_iota(jnp.int32, gates.shape, 1)
    pick = lambda c: jnp.sum(jnp.where(lane == c, gates, 0.0), axis=1, keepdims=True)
    ic = pick(d * 2 * nh + hd)
    fc = pick(d * 2 * nh + nh + hd)
    lf = jnp.minimum(fc, 0.0) - jnp.log(1.0 + jnp.exp(-jnp.abs(fc)))
    row = lax.broadcasted_iota(jnp.int32, (tc, tc), 0)
    col = lax.broadcasted_iota(jnp.int32, (tc, tc), 1)
    vis = ((col - row) * (1 - 2 * d)) <= 0
    tri = jnp.where(vis, 1.0, 0.0).astype(BF16)
    b1 = _dot_exact_lhs(tri, jnp.broadcast_to(lf, (tc, tc)))
    b2 = b1.T
    ic2 = jnp.broadcast_to(ic, (tc, tc)).T
    total = jnp.sum(lf, axis=0, keepdims=True)
    m_prev = m_scr[0:1, 0:1]
    logw = jnp.where(vis, b1 - b2 + ic2, NEG_INF)
    bcol = b1[:, 0:1]
    inter = bcol + m_prev
    m_row = jnp.maximum(jnp.max(logw, axis=1, keepdims=True), inter)
    q = q_ref[...]
    k = k_ref[...]
    v = v_ref[...]
    s = _dot_nt(q, k) * jnp.exp(logw - m_row)
    w_inter = jnp.exp(inter - m_row)
    c_old = c_scr[...]
    n_old = n_scr[0:1, :]
    num = _dot(s.astype(BF16), v) + w_inter * _dot(q, c_old.astype(BF16))
    den = (jnp.sum(s, axis=1, keepdims=True)
           + w_inter * jnp.sum(q.astype(F32) * n_old, axis=1, keepdims=True))
    o_ref[...] = (num / jnp.maximum(jnp.abs(den), jnp.exp(-m_row))).astype(o_ref.dtype)
    lws = total - bcol + ic
    m_new = jnp.maximum(total + m_prev, jnp.max(lws, axis=0, keepdims=True))
    ek = jnp.exp(lws - m_new) * k.astype(F32)
    cw = jnp.exp(total + m_prev - m_new)
    c_scr[...] = cw * c_old + _dot_tn(ek.astype(BF16), v)
    n_scr[0:1, :] = cw * n_old + jnp.sum(ek, axis=0, keepdims=True)
    m_scr[...] = jnp.broadcast_to(m_new, m_scr.shape)


def mlstm_mix(qk, uvo, gates, *, batch, lc, ctx, nh=M_HEADS, tc=M_CHUNK):
    n = qk.shape[0]
    mw = qk.shape[1] // 2
    dh = mw // nh
    nc = lc // tc
    nc_ctx = ctx // tc
    rb = lambda b, d, j: b * nc + _chunk_order(d, j, nc, nc_ctx)
    return pl.pallas_call(
        functools.partial(_mlstm_kernel, nh=nh),
        out_shape=jax.ShapeDtypeStruct((2, n, mw), F32),
        grid=(batch, nh, 2, nc),
        in_specs=[pl.BlockSpec((tc, dh), lambda b, h, d, j: (rb(b, d, j), h)),
                  pl.BlockSpec((tc, dh), lambda b, h, d, j: (rb(b, d, j), nh + h)),
                  pl.BlockSpec((tc, dh), lambda b, h, d, j: (rb(b, d, j), nh + h)),
                  pl.BlockSpec((tc, 128), lambda b, h, d, j: (rb(b, d, j), 0))],
        out_specs=pl.BlockSpec((None, tc, dh), lambda b, h, d, j: (d, rb(b, d, j), h)),
        scratch_shapes=[pltpu.VMEM((dh, dh), F32), pltpu.VMEM((8, dh), F32), pltpu.VMEM((8, 128), F32)],
        compiler_params=_cparams(("parallel", "parallel", "parallel", "arbitrary")),
        name="mlstm",
    )(qk, qk, uvo, gates)


def _even_out_kernel(x_ref, g1_ref, y_ref, h_ref, o_ref, mn_ref, wg_ref, bg_ref, wo_ref,
                     out_ref, mix_scr, *, tm, tpb, ctx, nh):
    i = pl.program_id(0)
    j = pl.program_id(1)

    @pl.when(j == 0)
    def _():
        s = _gelu_tanh(y_ref[...].astype(F32))
        glu = s * _sigmoid(_dot(s.astype(BF16), wg_ref[...]) + bg_ref[...])
        hm = h_ref[0] + h_ref[1]
        dh = hm.shape[1] // nh
        parts = []
        for hd in range(nh):
            seg = hm[:, hd * dh:(hd + 1) * dh]
            parts.append(seg * lax.rsqrt(jnp.mean(seg * seg, axis=-1, keepdims=True) + EPS))
        ml = jnp.concatenate(parts, axis=1) * mn_ref[...] * _sigmoid(o_ref[...].astype(F32))
        mix_scr[...] = jnp.concatenate([glu, ml], axis=1).astype(BF16)

    b = i // tpb
    is_ctx = ((i % tpb) * tm + _row_iota(tm)) < ctx
    out_ref[...] = x_ref[...] + _mod_rows(g1_ref, b, is_ctx) * _dot(mix_scr[...], wo_ref[...])


def even_out_block(x, mod, ys5, hdir, uvo, ml_norm, w_glu, b_glu, w_out, *, lc, ctx, tm, tn=512, nh=M_HEADS):
    n, d = x.shape
    sw = ys5.shape[1]
    mw = hdir.shape[2]
    nj = d // tn
    return pl.pallas_call(
        functools.partial(_even_out_kernel, tm=tm, tpb=lc // tm, ctx=ctx, nh=nh),
        out_shape=jax.ShapeDtypeStruct((n, d), F32),
        grid=(n // tm, nj),
        in_specs=[pl.BlockSpec((tm, tn), lambda i, j: (i, j)),
                  pl.BlockSpec((8, tn), lambda i, j: (0, 2 * nj + j)),
                  pl.BlockSpec((tm, sw), lambda i, j: (i, 0)),
                  pl.BlockSpec((2, tm, mw), lambda i, j: (0, i, 0)),
                  pl.BlockSpec((tm, mw), lambda i, j: (i, 2)),
                  pl.BlockSpec((1, mw), lambda i, j: (0, 0)),
                  pl.BlockSpec((sw, sw), lambda i, j: (0, 0)),
                  pl.BlockSpec((1, sw), lambda i, j: (0, 0)),
                  pl.BlockSpec((sw + mw, tn), lambda i, j: (0, j))],
        out_specs=pl.BlockSpec((tm, tn), lambda i, j: (i, j)),
        scratch_shapes=[pltpu.VMEM((tm, sw + mw), BF16)],
        compiler_params=_cparams(("parallel", "arbitrary")),
        name="even_out",
    )(x, mod, ys5, hdir, uvo, ml_norm.reshape(1, mw), w_glu, b_glu.reshape(1, sw), w_out)


def even_layer(x, mod, nw, p, s5_ops, *, batch, lc, ctx, tm):
    qk = norm_proj(x, mod, nw, p["w_qk"], p["b_qk"], lc=lc, ctx=ctx, tm=tm, tn=512, out_dtype=BF16,
                   conv=(p["conv_w"], p["conv_b"], p["qk_scale"]))
    uvo = norm_proj(x, mod, nw, p["w_uvo"], p["b_uvo"], lc=lc, ctx=ctx, tm=tm, tn=512, out_dtype=BF16)
    gates = norm_proj(x, mod, nw, p["w_gate"], p["b_gate"], lc=lc, ctx=ctx, tm=tm, tn=128, out_dtype=F32)
    sw = p["w_glu"].shape[0]
    ys5 = s5_mix(uvo[:, :sw], s5_ops, batch=batch, lc=lc, ctx=ctx)
    hdir = mlstm_mix(qk, uvo, gates, batch=batch, lc=lc, ctx=ctx)
    return even_out_block(x, mod, ys5, hdir, uvo, p["ml_norm"], p["w_glu"], p["b_glu"], p["w_out"],
                          lc=lc, ctx=ctx, tm=tm)


def even_params(w_in, b_in, w_out, w_glu, b_glu, conv_w, conv_b, ml_norm, nh=M_HEADS):
    sw = w_glu.shape[0]
    mw = ml_norm.shape[0]
    c0, c1, c2 = sw, sw + 2 * mw, sw + 4 * mw
    ng = w_in.shape[1] - c2
    scale = jnp.concatenate([jnp.ones((mw,), F32), jnp.full((mw,), (mw // nh) ** -0.5, F32)])
    uvo_cols = lambda a: jnp.concatenate([a[..., :c0], a[..., c1:c2]], axis=-1)
    return dict(
        w_qk=w_in[:, c0:c1].astype(BF16), b_qk=b_in[c0:c1], qk_scale=scale,
        w_uvo=uvo_cols(w_in).astype(BF16), b_uvo=uvo_cols(b_in),
        w_gate=jnp.pad(w_in[:, c2:], ((0, 0), (0, 128 - ng))).astype(BF16),
        b_gate=jnp.pad(b_in[c2:], (0, 128 - ng)),
        conv_w=conv_w, conv_b=conv_b, ml_norm=ml_norm,
        w_glu=w_glu.astype(BF16), b_glu=b_glu, w_out=w_out.astype(BF16))


_LH_W, _LH_A, _LH_G, _LH_V, _LH_END = 0, 256, 512, 768, 896


def _rw_proj_kernel(xm_ref, xp_ref, xn_ref, sh_ref, sc_ref, nw_ref, mu_ref, wl_ref, w_ref,
                    rkv_ref, lh_ref, mix_scr, *, tm, tpb, ctx, lc, nb):
    i = pl.program_id(0)
    j = pl.program_id(1)
    halo = GRID_W

    @pl.when(j == 0)
    def _():
        he, _ = _ext_rows(xm_ref, xp_ref, xn_ref, sh_ref, sc_ref, nw_ref, i, tm, tpb, ctx, halo)
        n = tm + 2 * halo
        d = he.shape[1]
        q = d // 4
        h = he[halo:halo + tm]
        hprev = pltpu.roll(he, 1, 0)[halo:halo + tm]
        hnext = pltpu.roll(he, n - 1, 0)[halo:halo + tm]
        hup = he[0:tm]
        hdown = he[2 * halo:2 * halo + tm]
        pos = (i % tpb) * tm + _row_iota(tm)
        is_ctx = pos < ctx
        pl_ = pos - ctx
        gcol = pl_ & (GRID_W - 1)
        ok_prev = jnp.where(is_ctx, pos, gcol) != 0
        ok_q1 = jnp.where(is_ctx, pos, gcol - (GRID_W - 1)) != 0
        ok_q2 = jnp.where(is_ctx, pos - (ctx - 1), jnp.maximum(pl_ - (GRID_W - 1), 0)) != 0
        ok_q3 = jnp.where(is_ctx, pos - (ctx - 1), jnp.maximum((lc - ctx) - GRID_W - pl_, 0)) != 0
        s0 = jnp.where(ok_prev, hprev[:, :q], 0.0)
        s1 = jnp.where(ok_q1, jnp.where(is_ctx, hprev[:, q:2 * q], hnext[:, q:2 * q]), 0.0)
        s2 = jnp.where(ok_q2, jnp.where(is_ctx, hnext[:, 2 * q:3 * q], hup[:, 2 * q:3 * q]), 0.0)
        s3 = jnp.where(ok_q3, jnp.where(is_ctx, hnext[:, 3 * q:], hdown[:, 3 * q:]), 0.0)
        xx = jnp.concatenate([s0, s1, s2, s3], axis=1) - h
        mix = lambda r: (h + xx * mu_ref[r:r + 1, :]).astype(BF16)
        xv = mix(3)
        mix_scr[0] = mix(0)
        mix_scr[1] = mix(2)
        mix_scr[2] = xv
        lh_ref[:, _LH_W:_LH_A] = jnp.tanh(_dot(mix(1), wl_ref[:, _LH_W:_LH_A]))
        lh_ref[:, _LH_A:_LH_G] = _dot(mix(4), wl_ref[:, _LH_A:_LH_G])
        lh_ref[:, _LH_G:_LH_V] = _sigmoid(_dot(mix(5), wl_ref[:, _LH_G:_LH_V]))
        lh_ref[:, _LH_V:_LH_END] = _dot(xv, wl_ref[:, _LH_V:_LH_END])

    rkv_ref[...] = _dot(mix_scr[j // nb], w_ref[...])


def rw_project(x, mod, nw, mu, w_lora1, w_rkv, *, lc, ctx, tm=256, tn=512):
    n, d = x.shape
    nb = d // tn
    return pl.pallas_call(
        functools.partial(_rw_proj_kernel, tm=tm, tpb=lc // tm, ctx=ctx, lc=lc, nb=nb),
        out_shape=[jax.ShapeDtypeStruct((3, n, d), F32), jax.ShapeDtypeStruct((n, _LH_END), F32)],
        grid=(n // tm, 3 * nb),
        in_specs=_halo_specs(tm, d, n, GRID_W) + [
            _mod_spec(d, 0), _mod_spec(d, 1), pl.BlockSpec((1, d), lambda i, j: (0, 0)),
            pl.BlockSpec((6, d), lambda i, j: (0, 0)),
            pl.BlockSpec((d, _LH_END), lambda i, j: (0, 0)),
            pl.BlockSpec((None, d, tn), lambda i, j: (j // nb, 0, j % nb))],
        out_specs=[pl.BlockSpec((None, tm, tn), lambda i, j: (j // nb, i, j % nb)),
                   pl.BlockSpec((tm, _LH_END), lambda i, j: (i, 0))],
        scratch_shapes=[pltpu.VMEM((3, tm, d), BF16)],
        compiler_params=_cparams(("parallel", "arbitrary")),
        name="rw_project",
    )(x, x, x, mod, mod, nw.reshape(1, d), mu, w_lora1, w_rkv)


def _head_sums(x, bd):
    hi = x.astype(BF16)
    lo = (x - hi.astype(F32)).astype(BF16)
    parts = []
    for c in range(x.shape[1] // 128):
        sl = slice(c * 128, (c + 1) * 128)
        parts.append(_dot(hi[:, sl], bd) + _dot(lo[:, sl], bd))
    return jnp.concatenate(parts, axis=1)


def _head_bd():
    r = lax.broadcasted_iota(jnp.int32, (128, 128), 0) // R_HEAD
    c = lax.broadcasted_iota(jnp.int32, (128, 128), 1) // R_HEAD
    return jnp.where(r == c, 1.0, 0.0).astype(BF16)


def _rw_gate_kernel(*refs, tm, has_vfirst):
    if has_vfirst:
        (rkv_ref, lh_ref, vf_ref, w2_ref, a2_ref, g2_ref, v2_ref, pv_ref,
         t6_ref, gt_ref, v_ref, aux_ref) = refs
    else:
        rkv_ref, lh_ref, w2_ref, a2_ref, g2_ref, pv_ref, t6_ref, gt_ref, v_ref, aux_ref = refs
    r = rkv_ref[0]
    k = rkv_ref[1]
    v = rkv_ref[2]
    pv = pv_ref[...]
    seg = lambda a, b: lh_ref[:, a:b].astype(BF16)
    if has_vfirst:
        v = v + (vf_ref[...] - v) * _sigmoid(pv[4:5] + _dot(seg(_LH_V, _LH_END), v2_ref[...]))
    v_ref[...] = v
    bd = _head_bd()
    kk = k * pv[5:6]
    kk = kk * lax.rsqrt(jnp.maximum(_head_sums(kk * kk, bd), 1e-24))
    aux_ref[1] = _dot(seg(_LH_G, _LH_V), g2_ref[...]).astype(aux_ref.dtype)
    row = lax.broadcasted_iota(jnp.int32, (tm, tm), 0)
    col = lax.broadcasted_iota(jnp.int32, (tm, tm), 1)
    same = (row // R_CHUNK) == (col // R_CHUNK)
    ones_bd = jnp.where(same, 1.0, 0.0).astype(BF16)
    hw = seg(_LH_W, _LH_A)
    ha = seg(_LH_A, _LH_G)
    ksum = jnp.zeros_like(k)
    nchunk = tm // R_CHUNK
    for d in range(2):
        wlog = -_softplus(-(pv[d:d + 1] + _dot(hw, w2_ref[d]))) - 0.5
        lw = -jnp.exp(wlog)
        a = _sigmoid(pv[2 + d:3 + d] + _dot(ha, a2_ref[d]))
        kd = k * (1.0 + (a - 1.0) * pv[6:7])
        bv = kk * a
        ksum = ksum + kd
        tri = jnp.where(same & ((col <= row) if d == 0 else (col >= row)), 1.0, 0.0).astype(BF16)
        cum = _dot_exact_lhs(tri, lw)
        tot = _dot_exact_lhs(ones_bd, lw)
        e_pos = jnp.exp(cum)
        e_neg = jnp.exp(-cum)
        e_end = jnp.exp(tot - cum)
        t6_ref[d, 0] = (r * e_pos).astype(BF16)
        t6_ref[d, 1] = (-kk * jnp.exp(cum - lw)).astype(BF16)
        t6_ref[d, 2] = (kd * e_neg).astype(BF16)
        t6_ref[d, 3] = (bv * e_neg).astype(BF16)
        t6_ref[d, 4] = (kd * e_end).astype(BF16)
        t6_ref[d, 5] = (bv * e_end).astype(BF16)
        gt = jnp.exp(tot)
        for c in range(nchunk):
            gt_ref[d, c] = gt[c * R_CHUNK:c * R_CHUNK + 1, :]
    bonus = _head_sums(r * ksum * pv[7:8], bd) * v
    aux_ref[0] = bonus.astype(aux_ref.dtype)


def rw_gates(rkv, lh, v_first, w2, a2, g2, v2, pvec, *, tm=256, tc=1024):
    _, n, d = rkv.shape
    has_vf = v_first is not None
    tile = pl.BlockSpec((tm, tc), lambda i, j: (i, j))
    in_specs = [pl.BlockSpec((3, tm, tc), lambda i, j: (0, i, j)),
                pl.BlockSpec((tm, _LH_END), lambda i, j: (i, 0))]
    args = [rkv, lh]
    if has_vf:
        in_specs.append(tile)
        args.append(v_first)
    in_specs += [pl.BlockSpec((2, 256, tc), lambda i, j: (0, 0, j)),
                 pl.BlockSpec((2, 256, tc), lambda i, j: (0, 0, j)),
                 pl.BlockSpec((256, tc), lambda i, j: (0, j))]
    args += [w2, a2, g2]
    if has_vf:
        in_specs.append(pl.BlockSpec((128, tc), lambda i, j: (0, j)))
        args.append(v2)
    in_specs.append(pl.BlockSpec((8, tc), lambda i, j: (0, j)))
    args.append(pvec)
    nch = tm // R_CHUNK
    return pl.pallas_call(
        functools.partial(_rw_gate_kernel, tm=tm, has_vfirst=has_vf),
        out_shape=[jax.ShapeDtypeStruct((2, 6, n, d), BF16),
                   jax.ShapeDtypeStruct((2, n // R_CHUNK, 1, d), F32),
                   jax.ShapeDtypeStruct((n, d), F32),
                   jax.ShapeDtypeStruct((2, n, d), BF16)],
        grid=(n // tm, d // tc),
        in_specs=in_specs,
        out_specs=[pl.BlockSpec((2, 6, tm, tc), lambda i, j: (0, 0, i, j)),
                   pl.BlockSpec((2, nch, 1, tc), lambda i, j: (0, i, 0, j)),
                   tile,
                   pl.BlockSpec((2, tm, tc), lambda i, j: (0, i, j))],
        compiler_params=_cparams(("parallel", "parallel")),
        name="rw_gates",
    )(*args)


def _pair_stack(y, hi_lane):
    z = jnp.zeros_like(y)
    return jnp.concatenate([jnp.where(hi_lane, z, y), jnp.where(hi_lane, y, z)], axis=0)


def _rw_chunk_kernel(t6_ref, v_ref, gt_ref, ghq_ref, y0_ref, *, npair):
    d = pl.program_id(1)
    tc = R_CHUNK
    lane = lax.broadcasted_iota(jnp.int32, (tc, 128), 1)
    row = lax.broadcasted_iota(jnp.int32, (tc, 128), 0)
    hi_lane = lane >= R_HEAD
    rel = ((lane & (R_HEAD - 1)) - row) * (1 - 2 * d)
    strict = rel < 0
    incl = rel <= 0
    eye2 = jnp.where(rel == 0, 1.0, 0.0)

    def pm(a, y):
        return _dot(a.astype(BF16), _pair_stack(y.astype(BF16), hi_lane))

    def ktv(x, y):
        full = _dot_tn(x.astype(BF16), y.astype(BF16))
        return jnp.where(hi_lane, full[R_HEAD:], full[:R_HEAD])

    def body(p, carry):
        off = pl.multiple_of(p * 128, 128)
        sl = pl.ds(off, 128)
        rt = t6_ref[0, :, sl]
        at = t6_ref[1, :, sl]
        kt = t6_ref[2, :, sl]
        bt = t6_ref[3, :, sl]
        kh = t6_ref[4, :, sl]
        bh = t6_ref[5, :, sl]
        v = v_ref[:, sl].astype(BF16)
        kst = _pair_stack(kt, hi_lane)
        bst = _pair_stack(bt, hi_lane)
        lab = jnp.where(strict, _dot_nt(at, bst), 0.0)
        lak = jnp.where(strict, _dot_nt(at, kst), 0.0)
        ark = jnp.where(incl, _dot_nt(rt, kst), 0.0)
        arb = jnp.where(incl, _dot_nt(rt, bst), 0.0)
        inv = eye2 + lab
        pw = lab
        for _ in range(5):
            pw = pm(pw, pw)
            inv = inv + pm(inv, pw)
        m2 = pm(lak, v)
        w = pm(inv, at)
        u0 = pm(inv, m2)
        q = rt.astype(F32) + pm(arb, w)
        y0 = pm(ark, v) + pm(arb, u0)
        g = jnp.where(rel == 0, gt_ref[:, sl], 0.0) + ktv(bh, w)
        h = ktv(kh, v) + ktv(bh, u0)
        ghq_ref[0, :, sl] = g.astype(BF16)
        ghq_ref[1, :, sl] = h.astype(BF16)
        ghq_ref[2, :, sl] = q.astype(BF16)
        y0_ref[:, sl] = y0
        return carry

    lax.fori_loop(0, npair, body, 0)


def rw_chunk_ops(t6, v, gt):
    _, _, n, d = t6.shape
    tc = R_CHUNK
    return pl.pallas_call(
        functools.partial(_rw_chunk_kernel, npair=d // 128),
        out_shape=[jax.ShapeDtypeStruct((2, 3, n, d), BF16), jax.ShapeDtypeStruct((2, n, d), F32)],
        grid=(n // tc, 2),
        in_specs=[pl.BlockSpec((None, 6, tc, d), lambda c, e: (e, 0, c, 0)),
                  pl.BlockSpec((tc, d), lambda c, e: (c, 0)),
                  pl.BlockSpec((None, None, 1, d), lambda c, e: (e, c, 0, 0))],
        out_specs=[pl.BlockSpec((None, 3, tc, d), lambda c, e: (e, 0, c, 0)),
                   pl.BlockSpec((None, tc, d), lambda c, e: (e, c, 0))],
        compiler_params=_cparams(("parallel", "parallel")),
        name="rw_chunk_ops",
    )(t6, v, gt)


def _rw_scan_kernel(ghq_ref, y0_ref, y_ref, s_scr, *, npair):
    j = pl.program_id(2)

    @pl.when(j == 0)
    def _():
        s_scr[...] = jnp.zeros_like(s_scr)

    hi_lane = lax.broadcasted_iota(jnp.int32, (R_CHUNK, 128), 1) >= R_HEAD

    def body(p, carry):
        off = pl.multiple_of(p * 128, 128)
        sl = pl.ds(off, 128)
        s = s_scr[:, sl]
        s_hi = s.astype(BF16)
        s_lo = (s - s_hi.astype(F32)).astype(BF16)
        gq = jnp.concatenate([ghq_ref[0, :, sl], ghq_ref[2, :, sl]], axis=0)
        res = _dot(gq, _pair_stack(s_hi, hi_lane)) + _dot(gq, _pair_stack(s_lo, hi_lane))
        s_scr[:, sl] = res[:R_CHUNK] + ghq_ref[1, :, sl].astype(F32)
        y_ref[:, sl] = res[R_CHUNK:] + y0_ref[:, sl]
        return carry

    lax.fori_loop(0, npair, body, 0)


def rw_scan(ghq, y0, *, batch, lc, ctx):
    _, _, n, d = ghq.shape
    tc = R_CHUNK
    nc = lc // tc
    nc_ctx = ctx // tc
    rb = lambda b, e, j: b * nc + _chunk_order(e, j, nc, nc_ctx)
    return pl.pallas_call(
        functools.partial(_rw_scan_kernel, npair=d // 128),
        out_shape=jax.ShapeDtypeStruct((2, n, d), F32),
        grid=(batch, 2, nc),
        in_specs=[pl.BlockSpec((None, 3, tc, d), lambda b, e, j: (e, 0, rb(b, e, j), 0)),
                  pl.BlockSpec((None, tc, d), lambda b, e, j: (e, rb(b, e, j), 0))],
        out_specs=pl.BlockSpec((None, tc, d), lambda b, e, j: (e, rb(b, e, j), 0)),
        scratch_shapes=[pltpu.VMEM((tc, d), F32)],
        compiler_params=_cparams(("parallel", "parallel", "arbitrary")),
        name="rw_scan",
    )(ghq, y0)


def _rw_out_kernel(x_ref, g1_ref, y_ref, aux_ref, ln_ref, wo_ref, out_ref, z_scr, *, tm, tpb, ctx):
    i = pl.program_id(0)
    j = pl.program_id(1)

    @pl.when(j == 0)
    def _():
        bd = _head_bd()
        y = y_ref[0] + y_ref[1]
        mu = _head_sums(y, bd) * (1.0 / R_HEAD)
        yc = y - mu
        var = _head_sums(yc * yc, bd) * (1.0 / R_HEAD)
        z = yc * lax.rsqrt(var + R_LN_EPS) * ln_ref[0:1, :] + ln_ref[1:2, :]
        z_scr[...] = ((z + aux_ref[0].astype(F32)) * aux_ref[1].astype(F32)).astype(BF16)

    b = i // tpb
    is_ctx = ((i % tpb) * tm + _row_iota(tm)) < ctx
    out_ref[...] = x_ref[...] + _mod_rows(g1_ref, b, is_ctx) * _dot(z_scr[...], wo_ref[...])


def rw_out_block(x, mod, y, aux, ln_wb, w_o, *, lc, ctx, tm, tn=512):
    n, d = x.shape
    nj = d // tn
    return pl.pallas_call(
        functools.partial(_rw_out_kernel, tm=tm, tpb=lc // tm, ctx=ctx),
        out_shape=jax.ShapeDtypeStruct((n, d), F32),
        grid=(n // tm, nj),
        in_specs=[pl.BlockSpec((tm, tn), lambda i, j: (i, j)),
                  pl.BlockSpec((8, tn), lambda i, j: (0, 2 * nj + j)),
                  pl.BlockSpec((2, tm, d), lambda i, j: (0, i, 0)),
                  pl.BlockSpec((2, tm, d), lambda i, j: (0, i, 0)),
                  pl.BlockSpec((2, d), lambda i, j: (0, 0)),
                  pl.BlockSpec((d, tn), lambda i, j: (0, j))],
        out_specs=pl.BlockSpec((tm, tn), lambda i, j: (i, j)),
        scratch_shapes=[pltpu.VMEM((tm, d), BF16)],
        compiler_params=_cparams(("parallel", "arbitrary")),
        name="rw_out",
    )(x, mod, y, aux, ln_wb, w_o)


def odd_params(mu, w_r, w_k, w_v, w_o, w0, w1, w2, a0, a1, a2, g1, g2, k_k, k_a, r_k, ln_w, ln_b,
               v0=None, v1=None, v2=None):
    d = w_r.shape[0]
    dw, da, dg = w1.shape[-1], a1.shape[-1], g1.shape[-1]
    assert 2 * dw <= _LH_A - _LH_W and 2 * da <= _LH_G - _LH_A and dg <= _LH_V - _LH_G
    wl = jnp.zeros((d, _LH_END), F32)
    wl = wl.at[:, _LH_W:_LH_W + dw].set(w1[0]).at[:, _LH_W + dw:_LH_W + 2 * dw].set(w1[1])
    wl = wl.at[:, _LH_A:_LH_A + da].set(a1[0]).at[:, _LH_A + da:_LH_A + 2 * da].set(a1[1])
    wl = wl.at[:, _LH_G:_LH_G + dg].set(g1)
    w2p = jnp.zeros((2, 256, d), F32).at[0, :dw].set(w2[0]).at[1, dw:2 * dw].set(w2[1])
    a2p = jnp.zeros((2, 256, d), F32).at[0, :da].set(a2[0]).at[1, da:2 * da].set(a2[1])
    g2p = jnp.zeros((256, d), F32).at[:dg].set(g2)
    v2p = None
    vzero = jnp.zeros((d,), F32)
    if v1 is not None:
        dv = v1.shape[-1]
        assert dv <= _LH_END - _LH_V
        wl = wl.at[:, _LH_V:_LH_V + dv].set(v1)
        v2p = jnp.zeros((128, d), F32).at[:dv].set(v2).astype(BF16)
    pvec = jnp.stack([w0[0], w0[1], a0[0], a0[1], v0 if v0 is not None else vzero,
                      k_k, k_a, r_k.reshape(d)])
    return dict(mu=mu, w_lora1=wl.astype(BF16), w_rkv=jnp.stack([w_r, w_k, w_v]).astype(BF16),
                w2=w2p.astype(BF16), a2=a2p.astype(BF16), g2=g2p.astype(BF16), v2=v2p, pvec=pvec,
                ln_wb=jnp.stack([ln_w, ln_b]), w_o=w_o.astype(BF16))


def odd_layer(x, mod, nw, p, v_first, *, batch, lc, ctx, tm):
    rkv, lh = rw_project(x, mod, nw, p["mu"], p["w_lora1"], p["w_rkv"], lc=lc, ctx=ctx)
    t6, gt, v, aux = rw_gates(rkv, lh, v_first if p["v2"] is not None else None,
                              p["w2"], p["a2"], p["g2"], p["v2"], p["pvec"])
    ghq, y0 = rw_chunk_ops(t6, v, gt)
    y = rw_scan(ghq, y0, batch=batch, lc=lc, ctx=ctx)
    return rw_out_block(x, mod, y, aux, p["ln_wb"], p["w_o"], lc=lc, ctx=ctx, tm=tm), v


def _row_tile(lc, cap):
    return max(t for t in range(16, cap + 1, 16) if lc % t == 0)


def kernel(x, c, ctx, c_ctx, ada_w, ada_b, norm_mix, norm_ffn, ffn_w_up, ffn_conv_w, ffn_conv_b, ffn_w_down, norm_final, ev_w_in, ev_b_in, ev_w_out, s5_lam_re, s5_lam_im, s5_log_step, s5_b_re, s5_b_im, s5_c_re, s5_c_im, s5_d, s5_w_glu, s5_b_glu, ml_conv_w, ml_conv_b, ml_norm, rw_mu, rw_w_r, rw_w_k, rw_w_v, rw_w_o, rw_w0, rw_w1, rw_w2, rw_a0, rw_a1, rw_a2, rw_v0, rw_v1, rw_v2, rw_g1, rw_g2, rw_k_k, rw_k_a, rw_r_k, rw_ln_w, rw_ln_b):
    batch, seq, d = x.shape
    n_ctx = ctx.shape[1]
    lc = n_ctx + seq
    depth = ada_w.shape[0]
    n_even = ev_w_in.shape[0]
    assert batch < _CTX_ROW and n_ctx % 256 == 0 and seq % 256 == 0 and seq % GRID_W == 0
    tm = _row_tile(lc, 544)
    tm_small = _row_tile(lc, 272)

    xc = jnp.concatenate([ctx, x], axis=1).reshape(batch * lc, d)
    c8 = jnp.zeros((8, d), F32).at[:batch].set(c).at[_CTX_ROW].set(c_ctx)
    mods = adaln(c8, ada_w, ada_b)

    g, p_state = s5_lam_re.shape[-2:]
    flat = lambda a: a.reshape((n_even * 2,) + a.shape[2:])
    s5p = s5_params(flat(s5_lam_re), flat(s5_lam_im), s5_log_step.reshape(n_even * 2, g),
                    flat(s5_b_re), flat(s5_b_im), flat(s5_c_re), flat(s5_c_im))

    v_first = None
    for l in range(depth):
        j = l // 2
        if l % 2 == 0:
            ep = even_params(ev_w_in[j], ev_b_in[j], ev_w_out[j], s5_w_glu[j], s5_b_glu[j],
                             ml_conv_w[j], ml_conv_b[j], ml_norm[j])
            ops = s5_assemble(tuple(a[2 * j:2 * j + 2] for a in s5p), s5_d[j])
            xc = even_layer(xc, mods[l], norm_mix[l], ep, ops, batch=batch, lc=lc, ctx=n_ctx, tm=tm)
        else:
            extra = {} if j == 0 else dict(v0=rw_v0[j - 1], v1=rw_v1[j - 1], v2=rw_v2[j - 1])
            op = odd_params(rw_mu[j], rw_w_r[j], rw_w_k[j], rw_w_v[j], rw_w_o[j], rw_w0[j], rw_w1[j],
                            rw_w2[j], rw_a0[j], rw_a1[j], rw_a2[j], rw_g1[j], rw_g2[j], rw_k_k[j],
                            rw_k_a[j], rw_r_k[j], rw_ln_w[j], rw_ln_b[j], **extra)
            xc, v = odd_layer(xc, mods[l], norm_mix[l], op, v_first, batch=batch, lc=lc, ctx=n_ctx,
                              tm=tm_small)
            if j == 0:
                v_first = v
        xc = conv_ffn_block(xc, mods[l], norm_ffn[l], ffn_w_up[l].astype(BF16), ffn_conv_w[l],
                            ffn_conv_b[l], ffn_w_down[l].astype(BF16), lc=lc, ctx=n_ctx, tm=tm)
    out = final_norm(xc, norm_final, batch=batch, lc=lc, ctx=n_ctx)
    return out.reshape(batch, seq, d)
```

```python
import functools

import jax
import jax.numpy as jnp
from jax import lax
from jax.experimental import pallas as pl
from jax.experimental.pallas import tpu as pltpu

F32 = jnp.float32
BF16 = jnp.bfloat16
EPS = 1e-6
NEG_INF = -1e30
GRID_W = 64
S5_T = 16
M_HEADS = 4
M_CHUNK = 128
R_HEAD = 64
R_CHUNK = 64
R_LN_EPS = 64e-5
HIGHEST = lax.Precision.HIGHEST
VMEM_LIMIT = 56 * 1024 * 1024


def _cparams(sem):
    return pltpu.CompilerParams(dimension_semantics=sem, vmem_limit_bytes=VMEM_LIMIT)


def _dot(a, b):
    return jnp.dot(a, b, preferred_element_type=F32)


def _dot_nt(a, b):
    return lax.dot_general(a, b, (((1,), (1,)), ((), ())), preferred_element_type=F32)


def _dot_tn(a, b):
    return lax.dot_general(a, b, (((0,), (0,)), ((), ())), preferred_element_type=F32)


def _split3(x):
    h = x.astype(BF16)
    r = x - h.astype(F32)
    m = r.astype(BF16)
    l = (r - m.astype(F32)).astype(BF16)
    return h, m, l


def _dot_exact_lhs(a_bf16, x):
    h, m, l = _split3(x)
    return _dot(a_bf16, h) + _dot(a_bf16, m) + _dot(a_bf16, l)


def _dot_exact_rhs(x, b_bf16):
    h, m, l = _split3(x)
    return _dot(h, b_bf16) + _dot(m, b_bf16) + _dot(l, b_bf16)


def _sigmoid(x):
    return jax.nn.sigmoid(x)


def _silu(x):
    return x * jax.nn.sigmoid(x)


def _gelu_tanh(x):
    return 0.5 * x * (1.0 + jnp.tanh(0.7978845608028654 * (x + 0.044715 * (x * x * x))))


def _softplus(x):
    return jnp.maximum(x, 0.0) + jnp.log(1.0 + jnp.exp(-jnp.abs(x)))


def _row_iota(n):
    return lax.broadcasted_iota(jnp.int32, (n, 1), 0)


def _mod_rows(ref, b, is_ctx):
    lat = ref[pl.ds(b, 1), :]
    ctx = ref[pl.ds(_CTX_ROW, 1), :]
    return jnp.where(is_ctx, ctx, lat)


_CTX_ROW = 7


def _norm_mod(x, nw, sc, sh):
    ms = jnp.mean(x * x, axis=-1, keepdims=True)
    return x * lax.rsqrt(ms + EPS) * nw * (1.0 + sc) + sh


def _adaln_kernel(c_ref, w_ref, b_ref, o_ref):
    a = _silu(c_ref[...])
    o_ref[0] = _dot(a.astype(BF16), w_ref[0].astype(BF16)) + b_ref[0]


def adaln(c8, ada_w, ada_b, tn=1024):
    depth, d, n6 = ada_w.shape
    return pl.pallas_call(
        _adaln_kernel,
        out_shape=jax.ShapeDtypeStruct((depth, 8, n6), F32),
        grid=(depth, n6 // tn),
        in_specs=[pl.BlockSpec((8, d), lambda l, j: (0, 0)),
                  pl.BlockSpec((1, d, tn), lambda l, j: (l, 0, j)),
                  pl.BlockSpec((1, 1, tn), lambda l, j: (l, 0, j))],
        out_specs=pl.BlockSpec((1, 8, tn), lambda l, j: (l, 0, j)),
        compiler_params=_cparams(("parallel", "parallel")),
        name="adaln",
    )(c8, ada_w, ada_b.reshape(depth, 1, n6))


def _halo_specs(tm, d, n_rows, halo):
    r = tm // halo
    nblk = n_rows // halo
    return [pl.BlockSpec((tm, d), lambda i, j: (i, 0)),
            pl.BlockSpec((halo, d), lambda i, j: (jnp.maximum(i * r - 1, 0), 0)),
            pl.BlockSpec((halo, d), lambda i, j: (jnp.minimum(i * r + r, nblk - 1), 0))]


def _mod_spec(d, k):
    return pl.BlockSpec((8, d), lambda i, j: (0, k))


def _conv3(u, first, last, cw, cb):
    n = u.shape[0]
    up = jnp.where(first, 0.0, pltpu.roll(u, 1, 0))
    un = jnp.where(last, 0.0, pltpu.roll(u, n - 1, 0))
    return up * cw[0:1] + u * cw[1:2] + un * cw[2:3] + cb


def _ext_rows(xm_ref, xp_ref, xn_ref, sh_ref, sc_ref, nw_ref, i, tm, tpb, ctx, halo):
    b = i // tpb
    pos = (i % tpb) * tm - halo + _row_iota(tm + 2 * halo)
    is_ctx = pos < ctx
    xe = jnp.concatenate([xp_ref[...], xm_ref[...], xn_ref[...]], axis=0)
    h = _norm_mod(xe, nw_ref[...], _mod_rows(sc_ref, b, is_ctx), _mod_rows(sh_ref, b, is_ctx))
    return h, pos


def _ffn_kernel(xm_ref, xp_ref, xn_ref, sh_ref, sc_ref, g_ref, nw_ref, wa_ref, wg_ref,
                cwa_ref, cwg_ref, cba_ref, cbg_ref, wd_ref, o_ref, h_scr, acc_scr,
                *, tm, tpb, ctx, lc):
    i = pl.program_id(0)
    j = pl.program_id(1)

    @pl.when(j == 0)
    def _():
        h, _ = _ext_rows(xm_ref, xp_ref, xn_ref, sh_ref, sc_ref, nw_ref, i, tm, tpb, ctx, 8)
        h_scr[...] = h.astype(BF16)
        acc_scr[...] = jnp.zeros_like(acc_scr)

    pos = (i % tpb) * tm - 8 + _row_iota(tm + 16)
    first = (pos == 0) | (pos == ctx)
    last = (pos == ctx - 1) | (pos == lc - 1)
    h = h_scr[...]
    a = _conv3(_dot(h, wa_ref[...]), first, last, cwa_ref[...], cba_ref[...])[8:8 + tm]
    g = _conv3(_dot(h, wg_ref[...]), first, last, cwg_ref[...], cbg_ref[...])[8:8 + tm]
    act = (a * _silu(g)).astype(BF16)
    acc_scr[...] += _dot(act, wd_ref[...])

    @pl.when(j == pl.num_programs(1) - 1)
    def _():
        b = i // tpb
        is_ctx = ((i % tpb) * tm + _row_iota(tm)) < ctx
        o_ref[...] = xm_ref[...] + _mod_rows(g_ref, b, is_ctx) * acc_scr[...]


def conv_ffn_block(x, mod, nw, w_up, conv_w, conv_b, w_down, *, lc, ctx, tm, tn=256):
    n, d = x.shape
    dff = w_down.shape[0]
    nj = dff // tn
    kern = functools.partial(_ffn_kernel, tm=tm, tpb=lc // tm, ctx=ctx, lc=lc)
    return pl.pallas_call(
        kern,
        out_shape=jax.ShapeDtypeStruct((n, d), F32),
        grid=(n // tm, nj),
        in_specs=_halo_specs(tm, d, n, 8) + [
            _mod_spec(d, 3), _mod_spec(d, 4), _mod_spec(d, 5),
            pl.BlockSpec((1, d), lambda i, j: (0, 0)),
            pl.BlockSpec((d, tn), lambda i, j: (0, j)),
            pl.BlockSpec((d, tn), lambda i, j: (0, j + nj)),
            pl.BlockSpec((3, tn), lambda i, j: (0, j)),
            pl.BlockSpec((3, tn), lambda i, j: (0, j + nj)),
            pl.BlockSpec((1, tn), lambda i, j: (0, j)),
            pl.BlockSpec((1, tn), lambda i, j: (0, j + nj)),
            pl.BlockSpec((tn, d), lambda i, j: (j, 0)),
        ],
        out_specs=pl.BlockSpec((tm, d), lambda i, j: (i, 0)),
        scratch_shapes=[pltpu.VMEM((tm + 16, d), BF16), pltpu.VMEM((tm, d), F32)],
        compiler_params=_cparams(("parallel", "arbitrary")),
        name="conv_ffn",
    )(x, x, x, mod, mod, mod, nw.reshape(1, d), w_up, w_up, conv_w, conv_w,
      conv_b.reshape(1, -1), conv_b.reshape(1, -1), w_down)


def _final_norm_kernel(x_ref, w_ref, o_ref):
    x = x_ref[...]
    o_ref[...] = x * lax.rsqrt(jnp.mean(x * x, axis=-1, keepdims=True) + EPS) * w_ref[...]


def final_norm(x, w, *, batch, lc, ctx, tm=256):
    n, d = x.shape
    tpb = lc // tm
    skip = ctx // tm
    per = tpb - skip
    return pl.pallas_call(
        _final_norm_kernel,
        out_shape=jax.ShapeDtypeStruct((batch * per * tm, d), F32),
        grid=(batch, per),
        in_specs=[pl.BlockSpec((tm, d), lambda b, t: (b * tpb + skip + t, 0)),
                  pl.BlockSpec((1, d), lambda b, t: (0, 0))],
        out_specs=pl.BlockSpec((tm, d), lambda b, t: (b * per + t, 0)),
        compiler_params=_cparams(("parallel", "parallel")),
        name="final_norm",
    )(x, w.reshape(1, d))


def _normproj_kernel(*refs, tm, tpb, ctx, lc, conv):
    if conv:
        (xm_ref, xp_ref, xn_ref, sh_ref, sc_ref, nw_ref, w_ref, b_ref, cw_ref, cb_ref, s_ref,
         o_ref, h_scr) = refs
    else:
        xm_ref, sh_ref, sc_ref, nw_ref, w_ref, b_ref, o_ref, h_scr = refs
    i = pl.program_id(0)
    j = pl.program_id(1)

    @pl.when(j == 0)
    def _():
        if conv:
            h, _ = _ext_rows(xm_ref, xp_ref, xn_ref, sh_ref, sc_ref, nw_ref, i, tm, tpb, ctx, 8)
        else:
            b = i // tpb
            is_ctx = ((i % tpb) * tm + _row_iota(tm)) < ctx
            h = _norm_mod(xm_ref[...], nw_ref[...], _mod_rows(sc_ref, b, is_ctx),
                          _mod_rows(sh_ref, b, is_ctx))
        h_scr[...] = h.astype(BF16)

    u = _dot(h_scr[...], w_ref[...]) + b_ref[...]
    if conv:
        pos = (i % tpb) * tm - 8 + _row_iota(tm + 16)
        first = (pos == 0) | (pos == ctx)
        last = (pos == ctx - 1) | (pos == lc - 1)
        u = _silu(_conv3(u, first, last, cw_ref[...], cb_ref[...])[8:8 + tm]) * s_ref[...]
    o_ref[...] = u.astype(o_ref.dtype)


def norm_proj(x, mod, nw, w, bias, *, lc, ctx, tm, tn, out_dtype, conv=None):
    n, d = x.shape
    ncol = w.shape[1]
    kern = functools.partial(_normproj_kernel, tm=tm, tpb=lc // tm, ctx=ctx, lc=lc,
                             conv=conv is not None)
    col = lambda r: pl.BlockSpec((r, tn), lambda i, j: (0, j))
    if conv is not None:
        cw, cb, scale = conv
        in_specs = _halo_specs(tm, d, n, 8) + [
            _mod_spec(d, 0), _mod_spec(d, 1), pl.BlockSpec((1, d), lambda i, j: (0, 0)),
            pl.BlockSpec((d, tn), lambda i, j: (0, j)), col(1), col(3), col(1), col(1)]
        args = (x, x, x, mod, mod, nw.reshape(1, d), w, bias.reshape(1, ncol), cw,
                cb.reshape(1, ncol), scale.reshape(1, ncol))
        rows = tm + 16
    else:
        in_specs = [pl.BlockSpec((tm, d), lambda i, j: (i, 0)),
                    _mod_spec(d, 0), _mod_spec(d, 1), pl.BlockSpec((1, d), lambda i, j: (0, 0)),
                    pl.BlockSpec((d, tn), lambda i, j: (0, j)), col(1)]
        args = (x, mod, mod, nw.reshape(1, d), w, bias.reshape(1, ncol))
        rows = tm
    return pl.pallas_call(
        kern,
        out_shape=jax.ShapeDtypeStruct((n, ncol), out_dtype),
        grid=(n // tm, ncol // tn),
        in_specs=in_specs,
        out_specs=pl.BlockSpec((tm, tn), lambda i, j: (i, j)),
        scratch_shapes=[pltpu.VMEM((rows, d), BF16)],
        compiler_params=_cparams(("parallel", "arbitrary")),
        name="norm_proj_conv" if conv is not None else "norm_proj",
    )(*args)


def _s5_param_kernel(lr_ref, li_ref, ls_ref, bbr_ref, bbi_ref, cr_ref, ci_ref,
                     wre_ref, wim_ref, ere_ref, eim_ref, k_ref, lam_re_ref, lam_im_ref, *, t, ch):
    step = jnp.exp(ls_ref[0])
    lr = lr_ref[0]
    li = li_ref[0]
    p = lr.shape[-1]
    sr = lr * step
    si = li * step

    def power(tau):
        mag = jnp.exp(tau * sr)
        return mag * jnp.cos(tau * si), mag * jnp.sin(tau * si)

    ab_re, ab_im = power(1.0)
    den = lr * lr + li * li
    co_re = ((ab_re - 1.0) * lr + ab_im * li) / den
    co_im = (ab_im * lr - (ab_re - 1.0) * li) / den
    b_re = bbr_ref[0]
    b_im = bbi_ref[0]
    bb_re = co_re * b_re - co_im * b_im
    bb_im = co_re * b_im + co_im * b_re
    tile = lambda m: jnp.concatenate([m] * t, axis=0)
    tau = (lax.broadcasted_iota(jnp.int32, (t * ch, p), 0) // ch).astype(F32)
    pr, pi = power(tau)
    bbr_t, bbi_t = tile(bb_re), tile(bb_im)
    w_re = pr * bbr_t - pi * bbi_t
    w_im = pr * bbi_t + pi * bbr_t
    wre_ref[0] = w_re
    wim_ref[0] = w_im
    qr, qi = power(tau + 1.0)
    c_re = cr_ref[0]
    c_im = ci_ref[0]
    cr_t, ci_t = tile(c_re), tile(c_im)
    ere_ref[0] = cr_t * qr - ci_t * qi
    eim_ref[0] = -(cr_t * qi + ci_t * qr)
    k_ref[0] = (lax.dot_general(c_re, w_re, (((1,), (1,)), ((), ())), precision=HIGHEST,
                                preferred_element_type=F32)
                - lax.dot_general(c_im, w_im, (((1,), (1,)), ((), ())), precision=HIGHEST,
                                  preferred_element_type=F32))
    lt_re, lt_im = power(float(t))
    lam_re_ref[0] = lt_re
    lam_im_ref[0] = lt_im


def s5_params(lam_re, lam_im, log_step, b_re, b_im, c_re, c_im, t=S5_T):
    nd, g, p = lam_re.shape
    ch = c_re.shape[2]
    m = nd * g
    r3 = lambda a: a.reshape(m, 1, p)
    bt = lambda a: jnp.swapaxes(a, -1, -2).reshape(m, ch, p)
    vec = pl.BlockSpec((1, 1, p), lambda i: (i, 0, 0))
    mat = pl.BlockSpec((1, ch, p), lambda i: (i, 0, 0))
    big = pl.BlockSpec((1, t * ch, p), lambda i: (i, 0, 0))
    outs = pl.pallas_call(
        functools.partial(_s5_param_kernel, t=t, ch=ch),
        out_shape=[jax.ShapeDtypeStruct((m, t * ch, p), F32)] * 4
        + [jax.ShapeDtypeStruct((m, ch, t * ch), F32)] + [jax.ShapeDtypeStruct((m, 1, p), F32)] * 2,
        grid=(m,),
        in_specs=[vec, vec, pl.BlockSpec((1, 1, 1), lambda i: (i, 0, 0)), mat, mat, mat, mat],
        out_specs=[big] * 4 + [pl.BlockSpec((1, ch, t * ch), lambda i: (i, 0, 0)), vec, vec],
        compiler_params=_cparams(("parallel",)),
        name="s5_params",
    )(r3(lam_re), r3(lam_im), log_step.reshape(m, 1, 1), bt(b_re), bt(b_im),
      c_re.reshape(m, ch, p), c_im.reshape(m, ch, p))
    w_re, w_im, e_re, e_im, k, l_re, l_im = outs
    sh = lambda a: a.reshape(nd, g, t, ch, p)
    return (sh(w_re), sh(w_im), sh(e_re), sh(e_im), k.reshape(nd, g, ch, t, ch),
            l_re.reshape(nd, g, p), l_im.reshape(nd, g, p))


def s5_assemble(params, d_skip, t=S5_T):
    w_re, w_im, e_re, e_im, k, l_re, l_im = params
    nd, g, _, ch, p = w_re.shape
    tc = t * ch
    s_i = jnp.arange(t)[:, None]
    t_i = jnp.arange(t)[None, :]
    kf = jnp.take(k[0], jnp.clip(t_i - s_i, 0, t - 1), axis=2)
    kb = jnp.take(k[1], jnp.clip(s_i - t_i, 0, t - 1), axis=2)
    mf = jnp.where((t_i >= s_i)[None, None, :, :, None], kf, 0.0)
    mb = jnp.where((s_i >= t_i)[None, None, :, :, None], kb, 0.0)
    m = jnp.transpose(mf + mb, (0, 2, 4, 3, 1))
    eye = (jnp.eye(t)[:, None, :, None] * jnp.eye(ch)[None, :, None, :])
    m = m + eye[None] * d_skip.reshape(g, 1, 1, 1, ch)
    m = m.reshape(g, tc, tc)
    def fmat(wr, wi):
        wr = wr.reshape(g, tc, p)
        wi = wi.reshape(g, tc, p)
        return jnp.concatenate([wr, wi, wi, wr], axis=-1)
    f2 = jnp.concatenate([fmat(w_re[0][:, ::-1], w_im[0][:, ::-1]), fmat(w_re[1], w_im[1])], axis=-1)
    def emat(er, ei):
        return jnp.concatenate([jnp.swapaxes(er.reshape(g, tc, p), 1, 2),
                                jnp.swapaxes(ei.reshape(g, tc, p), 1, 2)], axis=1)
    e2 = jnp.concatenate([emat(e_re[0], e_im[0]), emat(e_re[1][:, ::-1], e_im[1][:, ::-1])], axis=1)
    def coef(lr, li):
        a = jnp.concatenate([lr, lr], axis=-1)
        b1 = jnp.concatenate([-li, li], axis=-1)
        b2 = jnp.concatenate([li, -li], axis=-1)
        return jnp.concatenate([a, a], axis=-1), jnp.concatenate([b1, b2], axis=-1)
    af, bf = coef(l_re[0], l_im[0])
    ab, bb = coef(l_re[1], l_im[1])
    coefs = jnp.stack([jnp.concatenate([af, ab], axis=-1), jnp.concatenate([bf, bb], axis=-1)], axis=1)
    return m.astype(BF16), f2.astype(BF16), e2.astype(BF16), coefs


def _s5_in_kernel(u_ref, f_ref, z_ref):
    z_ref[...] = _dot(u_ref[0], f_ref[0])


def s5_chunk_inputs(ut, f2):
    g, r, tc = ut.shape
    w = f2.shape[-1]
    return pl.pallas_call(
        _s5_in_kernel,
        out_shape=jax.ShapeDtypeStruct((r, g * w), F32),
        grid=(g,),
        in_specs=[pl.BlockSpec((1, r, tc), lambda i: (i, 0, 0)),
                  pl.BlockSpec((1, tc, w), lambda i: (i, 0, 0))],
        out_specs=pl.BlockSpec((r, w), lambda i: (0, i)),
        compiler_params=_cparams(("parallel",)),
        name="s5_chunk_inputs",
    )(ut, f2)


def _s5_scan_kernel(z_ref, c_ref, h_ref, *, gb, nc, nc_ctx, lw):
    nb = z_ref.shape[0]
    ng, ng_ctx = nc // 8, nc_ctx // 8

    def body(jg, carry):
        rows = (pl.multiple_of(jg * 8, 8),
                pl.multiple_of(_chunk_order(1, jg, ng, ng_ctx) * 8, 8))
        new = list(carry)
        for gi in range(gb):
            for di in range(2):
                idx = 2 * (2 * gi + di)
                hs, hx = new[idx], new[idx + 1]
                base = (gi * 4 + 2 * di) * lw
                z8 = z_ref[:, pl.ds(rows[di], 8), base:base + lw]
                zx8 = z_ref[:, pl.ds(rows[di], 8), base + lw:base + 2 * lw]
                a = c_ref[gi, 0:1, 2 * di * lw:(2 * di + 1) * lw]
                b1 = c_ref[gi, 1:2, 2 * di * lw:(2 * di + 1) * lw]
                b2 = c_ref[gi, 1:2, (2 * di + 1) * lw:(2 * di + 2) * lw]
                entry = [None] * 8
                for s in range(8):
                    r = s if di == 0 else 7 - s
                    entry[r] = hs
                    hs, hx = (a * hs + b1 * hx + z8[:, r:r + 1, :],
                              a * hx + b2 * hs + zx8[:, r:r + 1, :])
                new[idx], new[idx + 1] = hs, hx
                h_ref[:, pl.ds(rows[di], 8), (gi * 2 + di) * lw:(gi * 2 + di + 1) * lw] = (
                    jnp.concatenate(entry, axis=1))
        return tuple(new)

    init = tuple(jnp.zeros((nb, 1, lw), F32) for _ in range(4 * gb))
    lax.fori_loop(0, ng, body, init)


def s5_chunk_scan(z, coefs, *, batch, nc, nc_ctx, gb=2):
    r, wtot = z.shape
    g = coefs.shape[0]
    lw = wtot // g // 4
    assert nc % 8 == 0 and nc_ctx % 8 == 0
    z3 = z.reshape(batch, nc, wtot)
    return pl.pallas_call(
        functools.partial(_s5_scan_kernel, gb=gb, nc=nc, nc_ctx=nc_ctx, lw=lw),
        out_shape=jax.ShapeDtypeStruct((batch, nc, g * 2 * lw), F32),
        grid=(g // gb,),
        in_specs=[pl.BlockSpec((batch, nc, gb * 4 * lw), lambda i: (0, 0, i)),
                  pl.BlockSpec((gb, 2, 4 * lw), lambda i: (i, 0, 0))],
        out_specs=pl.BlockSpec((batch, nc, gb * 2 * lw), lambda i: (0, 0, i)),
        compiler_params=_cparams(("parallel",)),
        name="s5_chunk_scan",
    )(z3, coefs).reshape(r, g * 2 * lw)


def _s5_out_kernel(u_ref, m_ref, h_ref, e_ref, y_ref):
    y = _dot(u_ref[0], m_ref[0]) + _dot(h_ref[...].astype(BF16), e_ref[0])
    y_ref[0] = y.astype(y_ref.dtype)


def s5_chunk_outputs(ut, m, hs, e2):
    g, r, tc = ut.shape
    hw = e2.shape[1]
    return pl.pallas_call(
        _s5_out_kernel,
        out_shape=jax.ShapeDtypeStruct((g, r, tc), BF16),
        grid=(g,),
        in_specs=[pl.BlockSpec((1, r, tc), lambda i: (i, 0, 0)),
                  pl.BlockSpec((1, tc, tc), lambda i: (i, 0, 0)),
                  pl.BlockSpec((r, hw), lambda i: (0, i)),
                  pl.BlockSpec((1, hw, tc), lambda i: (i, 0, 0))],
        out_specs=pl.BlockSpec((1, r, tc), lambda i: (i, 0, 0)),
        compiler_params=_cparams(("parallel",)),
        name="s5_chunk_outputs",
    )(ut, m, hs, e2)


def s5_mix(u, ops, *, batch, lc, ctx, t=S5_T):
    m, f2, e2, coefs = ops
    n, w = u.shape
    g = m.shape[0]
    ch = w // g
    r = n // t
    ut = jnp.transpose(u.reshape(r, t, g, ch), (2, 0, 1, 3)).reshape(g, r, t * ch)
    z = s5_chunk_inputs(ut, f2)
    hs = s5_chunk_scan(z, coefs, batch=batch, nc=lc // t, nc_ctx=ctx // t)
    yt = s5_chunk_outputs(ut, m, hs, e2)
    return jnp.transpose(yt.reshape(g, r, t, ch), (1, 2, 0, 3)).reshape(n, w)


def _chunk_order(d, j, nc, nc_ctx):
    bwd = jnp.where(j < nc_ctx, nc_ctx - 1 - j, nc - 1 - (j - nc_ctx))
    return jnp.where(d == 0, j, bwd)


def _mlstm_kernel(q_ref, k_ref, v_ref, g_ref, o_ref, c_scr, n_scr, m_scr, *, nh):
    hd = pl.program_id(1)
    d = pl.program_id(2)
    j = pl.program_id(3)
    tc = q_ref.shape[0]

    @pl.when(j == 0)
    def _():
        c_scr[...] = jnp.zeros_like(c_scr)
        n_scr[...] = jnp.zeros_like(n_scr)
        m_scr[...] = jnp.full_like(m_scr, NEG_INF)

    gates = g_ref[...]
    lane = lax.broadcasted_iota(jnp.int32, gates.shape, 1)
    pick = lambda c: jnp.sum(jnp.where(lane == c, gates, 0.0), axis=1, keepdims=True)
    ic = pick(d * 2 * nh + hd)
    fc = pick(d * 2 * nh + nh + hd)
    lf = jnp.minimum(fc, 0.0) - jnp.log(1.0 + jnp.exp(-jnp.abs(fc)))
    row = lax.broadcasted_iota(jnp.int32, (tc, tc), 0)
    col = lax.broadcasted_iota(jnp.int32, (tc, tc), 1)
    vis = ((col - row) * (1 - 2 * d)) <= 0
    tri = jnp.where(vis, 1.0, 0.0).astype(BF16)
    b1 = _dot_exact_lhs(tri, jnp.broadcast_to(lf, (tc, tc)))
    b2 = b1.T
    ic2 = jnp.broadcast_to(ic, (tc, tc)).T
    total = jnp.sum(lf, axis=0, keepdims=True)
    m_prev = m_scr[0:1, 0:1]
    logw = jnp.where(vis, b1 - b2 + ic2, NEG_INF)
    bcol = b1[:, 0:1]
    inter = bcol + m_prev
    m_row = jnp.maximum(jnp.max(logw, axis=1, keepdims=True), inter)
    q = q_ref[...]
    k = k_ref[...]
    v = v_ref[...]
    s = _dot_nt(q, k) * jnp.exp(logw - m_row)
    w_inter = jnp.exp(inter - m_row)
    c_old = c_scr[...]
    n_old = n_scr[0:1, :]
    num = _dot(s.astype(BF16), v) + w_inter * _dot(q, c_old.astype(BF16))
    den = (jnp.sum(s, axis=1, keepdims=True)
           + w_inter * jnp.sum(q.astype(F32) * n_old, axis=1, keepdims=True))
    o_ref[...] = (num / jnp.maximum(jnp.abs(den), jnp.exp(-m_row))).astype(o_ref.dtype)
    lws = total - bcol + ic
    m_new = jnp.maximum(total + m_prev, jnp.max(lws, axis=0, keepdims=True))
    ek = jnp.exp(lws - m_new) * k.astype(F32)
    cw = jnp.exp(total + m_prev - m_new)
    c_scr[...] = cw * c_old + _dot_tn(ek.astype(BF16), v)
    n_scr[0:1, :] = cw * n_old + jnp.sum(ek, axis=0, keepdims=True)
    m_scr[...] = jnp.broadcast_to(m_new, m_scr.shape)


def mlstm_mix(qk, uvo, gates, *, batch, lc, ctx, nh=M_HEADS, tc=M_CHUNK):
    n = qk.shape[0]
    mw = qk.shape[1] // 2
    dh = mw // nh
    nc = lc // tc
    nc_ctx = ctx // tc
    rb = lambda b, d, j: b * nc + _chunk_order(d, j, nc, nc_ctx)
    return pl.pallas_call(
        functools.partial(_mlstm_kernel, nh=nh),
        out_shape=jax.ShapeDtypeStruct((2, n, mw), F32),
        grid=(batch, nh, 2, nc),
        in_specs=[pl.BlockSpec((tc, dh), lambda b, h, d, j: (rb(b, d, j), h)),
                  pl.BlockSpec((tc, dh), lambda b, h, d, j: (rb(b, d, j), nh + h)),
                  pl.BlockSpec((tc, dh), lambda b, h, d, j: (rb(b, d, j), nh + h)),
                  pl.BlockSpec((tc, 128), lambda b, h, d, j: (rb(b, d, j), 0))],
        out_specs=pl.BlockSpec((None, tc, dh), lambda b, h, d, j: (d, rb(b, d, j), h)),
        scratch_shapes=[pltpu.VMEM((dh, dh), F32), pltpu.VMEM((8, dh), F32), pltpu.VMEM((8, 128), F32)],
        compiler_params=_cparams(("parallel", "parallel", "parallel", "arbitrary")),
        name="mlstm",
    )(qk, qk, uvo, gates)


def _even_out_kernel(x_ref, g1_ref, y_ref, h_ref, o_ref, mn_ref, wg_ref, bg_ref, wo_ref,
                     out_ref, mix_scr, *, tm, tpb, ctx, nh):
    i = pl.program_id(0)
    j = pl.program_id(1)

    @pl.when(j == 0)
    def _():
        s = _gelu_tanh(y_ref[...].astype(F32))
        glu = s * _sigmoid(_dot(s.astype(BF16), wg_ref[...]) + bg_ref[...])
        hm = h_ref[0] + h_ref[1]
        dh = hm.shape[1] // nh
        parts = []
        for hd in range(nh):
            seg = hm[:, hd * dh:(hd + 1) * dh]
            parts.append(seg * lax.rsqrt(jnp.mean(seg * seg, axis=-1, keepdims=True) + EPS))
        ml = jnp.concatenate(parts, axis=1) * mn_ref[...] * _sigmoid(o_ref[...].astype(F32))
        mix_scr[...] = jnp.concatenate([glu, ml], axis=1).astype(BF16)

    b = i // tpb
    is_ctx = ((i % tpb) * tm + _row_iota(tm)) < ctx
    out_ref[...] = x_ref[...] + _mod_rows(g1_ref, b, is_ctx) * _dot(mix_scr[...], wo_ref[...])


def even_out_block(x, mod, ys5, hdir, uvo, ml_norm, w_glu, b_glu, w_out, *, lc, ctx, tm, tn=512, nh=M_HEADS):
    n, d = x.shape
    sw = ys5.shape[1]
    mw = hdir.shape[2]
    nj = d // tn
    return pl.pallas_call(
        functools.partial(_even_out_kernel, tm=tm, tpb=lc // tm, ctx=ctx, nh=nh),
        out_shape=jax.ShapeDtypeStruct((n, d), F32),
        grid=(n // tm, nj),
        in_specs=[pl.BlockSpec((tm, tn), lambda i, j: (i, j)),
                  pl.BlockSpec((8, tn), lambda i, j: (0, 2 * nj + j)),
                  pl.BlockSpec((tm, sw), lambda i, j: (i, 0)),
                  pl.BlockSpec((2, tm, mw), lambda i, j: (0, i, 0)),
                  pl.BlockSpec((tm, mw), lambda i, j: (i, 2)),
                  pl.BlockSpec((1, mw), lambda i, j: (0, 0)),
                  pl.BlockSpec((sw, sw), lambda i, j: (0, 0)),
                  pl.BlockSpec((1, sw), lambda i, j: (0, 0)),
                  pl.BlockSpec((sw + mw, tn), lambda i, j: (0, j))],
        out_specs=pl.BlockSpec((tm, tn), lambda i, j: (i, j)),
        scratch_shapes=[pltpu.VMEM((tm, sw + mw), BF16)],
        compiler_params=_cparams(("parallel", "arbitrary")),
        name="even_out",
    )(x, mod, ys5, hdir, uvo, ml_norm.reshape(1, mw), w_glu, b_glu.reshape(1, sw), w_out)


def even_layer(x, mod, nw, p, s5_ops, *, batch, lc, ctx, tm):
    qk = norm_proj(x, mod, nw, p["w_qk"], p["b_qk"], lc=lc, ctx=ctx, tm=tm, tn=512, out_dtype=BF16,
                   conv=(p["conv_w"], p["conv_b"], p["qk_scale"]))
    uvo = norm_proj(x, mod, nw, p["w_uvo"], p["b_uvo"], lc=lc, ctx=ctx, tm=tm, tn=512, out_dtype=BF16)
    gates = norm_proj(x, mod, nw, p["w_gate"], p["b_gate"], lc=lc, ctx=ctx, tm=tm, tn=128, out_dtype=F32)
    sw = p["w_glu"].shape[0]
    ys5 = s5_mix(uvo[:, :sw], s5_ops, batch=batch, lc=lc, ctx=ctx)
    hdir = mlstm_mix(qk, uvo, gates, batch=batch, lc=lc, ctx=ctx)
    return even_out_block(x, mod, ys5, hdir, uvo, p["ml_norm"], p["w_glu"], p["b_glu"], p["w_out"],
                          lc=lc, ctx=ctx, tm=tm)


def even_params(w_in, b_in, w_out, w_glu, b_glu, conv_w, conv_b, ml_norm, nh=M_HEADS):
    sw = w_glu.shape[0]
    mw = ml_norm.shape[0]
    c0, c1, c2 = sw, sw + 2 * mw, sw + 4 * mw
    ng = w_in.shape[1] - c2
    scale = jnp.concatenate([jnp.ones((mw,), F32), jnp.full((mw,), (mw // nh) ** -0.5, F32)])
    uvo_cols = lambda a: jnp.concatenate([a[..., :c0], a[..., c1:c2]], axis=-1)
    return dict(
        w_qk=w_in[:, c0:c1].astype(BF16), b_qk=b_in[c0:c1], qk_scale=scale,
        w_uvo=uvo_cols(w_in).astype(BF16), b_uvo=uvo_cols(b_in),
        w_gate=jnp.pad(w_in[:, c2:], ((0, 0), (0, 128 - ng))).astype(BF16),
        b_gate=jnp.pad(b_in[c2:], (0, 128 - ng)),
        conv_w=conv_w, conv_b=conv_b, ml_norm=ml_norm,
        w_glu=w_glu.astype(BF16), b_glu=b_glu, w_out=w_out.astype(BF16))


_LH_W, _LH_A, _LH_G, _LH_V, _LH_END = 0, 256, 512, 768, 896


def _rw_proj_kernel(xm_ref, xp_ref, xn_ref, sh_ref, sc_ref, nw_ref, mu_ref, wl_ref, w_ref,
                    rkv_ref, lh_ref, mix_scr, *, tm, tpb, ctx, lc, nb):
    i = pl.program_id(0)
    j = pl.program_id(1)
    halo = GRID_W

    @pl.when(j == 0)
    def _():
        he, _ = _ext_rows(xm_ref, xp_ref, xn_ref, sh_ref, sc_ref, nw_ref, i, tm, tpb, ctx, halo)
        n = tm + 2 * halo
        d = he.shape[1]
        q = d // 4
        h = he[halo:halo + tm]
        hprev = pltpu.roll(he, 1, 0)[halo:halo + tm]
        hnext = pltpu.roll(he, n - 1, 0)[halo:halo + tm]
        hup = he[0:tm]
        hdown = he[2 * halo:2 * halo + tm]
        pos = (i % tpb) * tm + _row_iota(tm)
        is_ctx = pos < ctx
        pl_ = pos - ctx
        gcol = pl_ & (GRID_W - 1)
        ok_prev = jnp.where(is_ctx, pos, gcol) != 0
        ok_q1 = jnp.where(is_ctx, pos, gcol - (GRID_W - 1)) != 0
        ok_q2 = jnp.where(is_ctx, pos - (ctx - 1), jnp.maximum(pl_ - (GRID_W - 1), 0)) != 0
        ok_q3 = jnp.where(is_ctx, pos - (ctx - 1), jnp.maximum((lc - ctx) - GRID_W - pl_, 0)) != 0
        s0 = jnp.where(ok_prev, hprev[:, :q], 0.0)
        s1 = jnp.where(ok_q1, jnp.where(is_ctx, hprev[:, q:2 * q], hnext[:, q:2 * q]), 0.0)
        s2 = jnp.where(ok_q2, jnp.where(is_ctx, hnext[:, 2 * q:3 * q], hup[:, 2 * q:3 * q]), 0.0)
        s3 = jnp.where(ok_q3, jnp.where(is_ctx, hnext[:, 3 * q:], hdown[:, 3 * q:]), 0.0)
        xx = jnp.concatenate([s0, s1, s2, s3], axis=1) - h
        mix = lambda r: (h + xx * mu_ref[r:r + 1, :]).astype(BF16)
        xv = mix(3)
        mix_scr[0] = mix(0)
        mix_scr[1] = mix(2)
        mix_scr[2] = xv
        lh_ref[:, _LH_W:_LH_A] = jnp.tanh(_dot(mix(1), wl_ref[:, _LH_W:_LH_A]))
        lh_ref[:, _LH_A:_LH_G] = _dot(mix(4), wl_ref[:, _LH_A:_LH_G])
        lh_ref[:, _LH_G:_LH_V] = _sigmoid(_dot(mix(5), wl_ref[:, _LH_G:_LH_V]))
        lh_ref[:, _LH_V:_LH_END] = _dot(xv, wl_ref[:, _LH_V:_LH_END])

    rkv_ref[...] = _dot(mix_scr[j // nb], w_ref[...])


def rw_project(x, mod, nw, mu, w_lora1, w_rkv, *, lc, ctx, tm=256, tn=512):
    n, d = x.shape
    nb = d // tn
    return pl.pallas_call(
        functools.partial(_rw_proj_kernel, tm=tm, tpb=lc // tm, ctx=ctx, lc=lc, nb=nb),
        out_shape=[jax.ShapeDtypeStruct((3, n, d), F32), jax.ShapeDtypeStruct((n, _LH_END), F32)],
        grid=(n // tm, 3 * nb),
        in_specs=_halo_specs(tm, d, n, GRID_W) + [
            _mod_spec(d, 0), _mod_spec(d, 1), pl.BlockSpec((1, d), lambda i, j: (0, 0)),
            pl.BlockSpec((6, d), lambda i, j: (0, 0)),
            pl.BlockSpec((d, _LH_END), lambda i, j: (0, 0)),
            pl.BlockSpec((None, d, tn), lambda i, j: (j // nb, 0, j % nb))],
        out_specs=[pl.BlockSpec((None, tm, tn), lambda i, j: (j // nb, i, j % nb)),
                   pl.BlockSpec((tm, _LH_END), lambda i, j: (i, 0))],
        scratch_shapes=[pltpu.VMEM((3, tm, d), BF16)],
        compiler_params=_cparams(("parallel", "arbitrary")),
        name="rw_project",
    )(x, x, x, mod, mod, nw.reshape(1, d), mu, w_lora1, w_rkv)


def _head_sums(x, bd):
    hi = x.astype(BF16)
    lo = (x - hi.astype(F32)).astype(BF16)
    parts = []
    for c in range(x.shape[1] // 128):
        sl = slice(c * 128, (c + 1) * 128)
        parts.append(_dot(hi[:, sl], bd) + _dot(lo[:, sl], bd))
    return jnp.concatenate(parts, axis=1)


def _head_bd():
    r = lax.broadcasted_iota(jnp.int32, (128, 128), 0) // R_HEAD
    c = lax.broadcasted_iota(jnp.int32, (128, 128), 1) // R_HEAD
    return jnp.where(r == c, 1.0, 0.0).astype(BF16)


def _rw_gate_kernel(*refs, tm, has_vfirst):
    if has_vfirst:
        (rkv_ref, lh_ref, vf_ref, w2_ref, a2_ref, g2_ref, v2_ref, pv_ref,
         t6_ref, gt_ref, v_ref, aux_ref) = refs
    else:
        rkv_ref, lh_ref, w2_ref, a2_ref, g2_ref, pv_ref, t6_ref, gt_ref, v_ref, aux_ref = refs
    r = rkv_ref[0]
    k = rkv_ref[1]
    v = rkv_ref[2]
    pv = pv_ref[...]
    seg = lambda a, b: lh_ref[:, a:b].astype(BF16)
    if has_vfirst:
        v = v + (vf_ref[...] - v) * _sigmoid(pv[4:5] + _dot(seg(_LH_V, _LH_END), v2_ref[...]))
    v_ref[...] = v
    bd = _head_bd()
    kk = k * pv[5:6]
    kk = kk * lax.rsqrt(jnp.maximum(_head_sums(kk * kk, bd), 1e-24))
    aux_ref[1] = _dot(seg(_LH_G, _LH_V), g2_ref[...]).astype(aux_ref.dtype)
    row = lax.broadcasted_iota(jnp.int32, (tm, tm), 0)
    col = lax.broadcasted_iota(jnp.int32, (tm, tm), 1)
    same = (row // R_CHUNK) == (col // R_CHUNK)
    ones_bd = jnp.where(same, 1.0, 0.0).astype(BF16)
    hw = seg(_LH_W, _LH_A)
    ha = seg(_LH_A, _LH_G)
    ksum = jnp.zeros_like(k)
    nchunk = tm // R_CHUNK
    for d in range(2):
        wlog = -_softplus(-(pv[d:d + 1] + _dot(hw, w2_ref[d]))) - 0.5
        lw = -jnp.exp(wlog)
        a = _sigmoid(pv[2 + d:3 + d] + _dot(ha, a2_ref[d]))
        kd = k * (1.0 + (a - 1.0) * pv[6:7])
        bv = kk * a
        ksum = ksum + kd
        tri = jnp.where(same & ((col <= row) if d == 0 else (col >= row)), 1.0, 0.0).astype(BF16)
        cum = _dot_exact_lhs(tri, lw)
        tot = _dot_exact_lhs(ones_bd, lw)
        e_pos = jnp.exp(cum)
        e_neg = jnp.exp(-cum)
        e_end = jnp.exp(tot - cum)
        t6_ref[d, 0] = (r * e_pos).astype(BF16)
        t6_ref[d, 1] = (-kk * jnp.exp(cum - lw)).astype(BF16)
        t6_ref[d, 2] = (kd * e_neg).astype(BF16)
        t6_ref[d, 3] = (bv * e_neg).astype(BF16)
        t6_ref[d, 4] = (kd * e_end).astype(BF16)
        t6_ref[d, 5] = (bv * e_end).astype(BF16)
        gt = jnp.exp(tot)
        for c in range(nchunk):
            gt_ref[d, c] = gt[c * R_CHUNK:c * R_CHUNK + 1, :]
    bonus = _head_sums(r * ksum * pv[7:8], bd) * v
    aux_ref[0] = bonus.astype(aux_ref.dtype)


def rw_gates(rkv, lh, v_first, w2, a2, g2, v2, pvec, *, tm=256, tc=1024):
    _, n, d = rkv.shape
    has_vf = v_first is not None
    tile = pl.BlockSpec((tm, tc), lambda i, j: (i, j))
    in_specs = [pl.BlockSpec((3, tm, tc), lambda i, j: (0, i, j)),
                pl.BlockSpec((tm, _LH_END), lambda i, j: (i, 0))]
    args = [rkv, lh]
    if has_vf:
        in_specs.append(tile)
        args.append(v_first)
    in_specs += [pl.BlockSpec((2, 256, tc), lambda i, j: (0, 0, j)),
                 pl.BlockSpec((2, 256, tc), lambda i, j: (0, 0, j)),
                 pl.BlockSpec((256, tc), lambda i, j: (0, j))]
    args += [w2, a2, g2]
    if has_vf:
        in_specs.append(pl.BlockSpec((128, tc), lambda i, j: (0, j)))
        args.append(v2)
    in_specs.append(pl.BlockSpec((8, tc), lambda i, j: (0, j)))
    args.append(pvec)
    nch = tm // R_CHUNK
    return pl.pallas_call(
        functools.partial(_rw_gate_kernel, tm=tm, has_vfirst=has_vf),
        out_shape=[jax.ShapeDtypeStruct((2, 6, n, d), BF16),
                   jax.ShapeDtypeStruct((2, n // R_CHUNK, 1, d), F32),
                   jax.ShapeDtypeStruct((n, d), F32),
                   jax.ShapeDtypeStruct((2, n, d), BF16)],
        grid=(n // tm, d // tc),
        in_specs=in_specs,
        out_specs=[pl.BlockSpec((2, 6, tm, tc), lambda i, j: (0, 0, i, j)),
                   pl.BlockSpec((2, nch, 1, tc), lambda i, j: (0, i, 0, j)),
                   tile,
                   pl.BlockSpec((2, tm, tc), lambda i, j: (0, i, j))],
        compiler_params=_cparams(("parallel", "parallel")),
        name="rw_gates",
    )(*args)


def _pair_stack(y, hi_lane):
    z = jnp.zeros_like(y)
    return jnp.concatenate([jnp.where(hi_lane, z, y), jnp.where(hi_lane, y, z)], axis=0)


def _rw_chunk_kernel(t6_ref, v_ref, gt_ref, ghq_ref, y0_ref, *, npair, unroll):
    d = pl.program_id(1)
    tc = R_CHUNK
    lane = lax.broadcasted_iota(jnp.int32, (tc, 128), 1)
    row = lax.broadcasted_iota(jnp.int32, (tc, 128), 0)
    hi_lane = lane >= R_HEAD
    rel = ((lane & (R_HEAD - 1)) - row) * (1 - 2 * d)
    strict = rel < 0
    incl = rel <= 0
    eye2 = jnp.where(rel == 0, 1.0, 0.0)

    stack = lambda y: _pair_stack(y.astype(BF16), hi_lane)
    rows = lambda a, b: jnp.concatenate([a.astype(BF16), b.astype(BF16)], axis=0)
    top, bot = (lambda m: m[:tc]), (lambda m: m[tc:])
    lft, rgt = (lambda m: m[:, :128]), (lambda m: m[:, 128:])

    def pm(a, y):
        return _dot(a.astype(BF16), stack(y))

    def pm2(a, y1, y2):
        return _dot(a.astype(BF16), jnp.concatenate([stack(y1), stack(y2)], axis=1))

    def pack_kv(full):
        return jnp.where(hi_lane, full[R_HEAD:], full[:R_HEAD])

    def body(it, carry):
        sls = [pl.ds(pl.multiple_of((it * unroll + u) * 128, 128), 128) for u in range(unroll)]
        each = lambda f, *cols: [f(*xs) for xs in zip(*cols)]
        rt, at, kt, bt, kh, bh = ([t6_ref[c, :, sl] for sl in sls] for c in range(6))
        v = [v_ref[:, sl].astype(BF16) for sl in sls]
        ar = each(rows, at, rt)
        xb = each(lambda x, y: _dot_nt(x, stack(y)), ar, bt)
        xk = each(lambda x, y: _dot_nt(x, stack(y)), ar, kt)
        lab = each(lambda m: jnp.where(strict, top(m), 0.0), xb)
        arb = each(lambda m: jnp.where(incl, bot(m), 0.0), xb)
        lak = each(lambda m: jnp.where(strict, top(m), 0.0), xk)
        ark = each(lambda m: jnp.where(incl, bot(m), 0.0), xk)
        inv = each(lambda l: eye2 + l, lab)
        pw = each(pm, lab, lab)
        for _ in range(4):
            both = each(lambda i, p: pm(rows(i, p), p), inv, pw)
            inv = each(lambda i, m: i + top(m), inv, both)
            pw = each(bot, both)
        inv = each(lambda i, p: i + pm(i, p), inv, pw)
        lv = each(lambda a, b, vv: pm(rows(a, b), vv), lak, ark, v)
        m2 = each(top, lv)
        wu = each(pm2, inv, at, m2)
        au = each(lambda a, m: pm2(a, lft(m), rgt(m)), arb, wu)
        q = each(lambda r, m: r.astype(F32) + lft(m), rt, au)
        y0 = each(lambda m, n: bot(m) + rgt(n), lv, au)
        bwu = each(lambda b, m: _dot_tn(b, m.astype(BF16)), bh, wu)
        g = each(lambda sl, m: jnp.where(rel == 0, gt_ref[:, sl], 0.0) + pack_kv(lft(m)), sls, bwu)
        h = each(lambda k, vv, m: pack_kv(_dot_tn(k, vv)) + pack_kv(rgt(m)), kh, v, bwu)
        for u, sl in enumerate(sls):
            ghq_ref[0, :, sl] = g[u].astype(BF16)
            ghq_ref[1, :, sl] = h[u].astype(BF16)
            ghq_ref[2, :, sl] = q[u].astype(BF16)
            y0_ref[:, sl] = y0[u]
        return carry

    lax.fori_loop(0, npair // unroll, body, 0)


def rw_chunk_ops(t6, v, gt):
    _, _, n, d = t6.shape
    tc = R_CHUNK
    return pl.pallas_call(
        functools.partial(_rw_chunk_kernel, npair=d // 128, unroll=16),
        out_shape=[jax.ShapeDtypeStruct((2, 3, n, d), BF16), jax.ShapeDtypeStruct((2, n, d), F32)],
        grid=(n // tc, 2),
        in_specs=[pl.BlockSpec((None, 6, tc, d), lambda c, e: (e, 0, c, 0)),
                  pl.BlockSpec((tc, d), lambda c, e: (c, 0)),
                  pl.BlockSpec((None, None, 1, d), lambda c, e: (e, c, 0, 0))],
        out_specs=[pl.BlockSpec((None, 3, tc, d), lambda c, e: (e, 0, c, 0)),
                   pl.BlockSpec((None, tc, d), lambda c, e: (e, c, 0))],
        compiler_params=_cparams(("parallel", "parallel")),
        name="rw_chunk_ops",
    )(t6, v, gt)


def _rw_scan_kernel(ghq_ref, y0_ref, y_ref, s_scr, *, npair):
    j = pl.program_id(2)

    @pl.when(j == 0)
    def _():
        s_scr[...] = jnp.zeros_like(s_scr)

    hi_lane = lax.broadcasted_iota(jnp.int32, (R_CHUNK, 128), 1) >= R_HEAD

    for p in range(npair):
        sl = slice(p * 128, (p + 1) * 128)
        s = s_scr[:, sl]
        s_hi = s.astype(BF16)
        s_lo = (s - s_hi.astype(F32)).astype(BF16)
        gq = jnp.concatenate([ghq_ref[0, :, sl], ghq_ref[2, :, sl]], axis=0)
        res = _dot(gq, _pair_stack(s_hi, hi_lane)) + _dot(gq, _pair_stack(s_lo, hi_lane))
        s_scr[:, sl] = res[:R_CHUNK] + ghq_ref[1, :, sl].astype(F32)
        y_ref[:, sl] = res[R_CHUNK:] + y0_ref[:, sl]


def rw_scan(ghq, y0, *, batch, lc, ctx):
    _, _, n, d = ghq.shape
    tc = R_CHUNK
    nc = lc // tc
    nc_ctx = ctx // tc
    rb = lambda b, e, j: b * nc + _chunk_order(e, j, nc, nc_ctx)
    return pl.pallas_call(
        functools.partial(_rw_scan_kernel, npair=d // 128),
        out_shape=jax.ShapeDtypeStruct((2, n, d), F32),
        grid=(batch, 2, nc),
        in_specs=[pl.BlockSpec((None, 3, tc, d), lambda b, e, j: (e, 0, rb(b, e, j), 0)),
                  pl.BlockSpec((None, tc, d), lambda b, e, j: (e, rb(b, e, j), 0))],
        out_specs=pl.BlockSpec((None, tc, d), lambda b, e, j: (e, rb(b, e, j), 0)),
        scratch_shapes=[pltpu.VMEM((tc, d), F32)],
        compiler_params=_cparams(("parallel", "parallel", "arbitrary")),
        name="rw_scan",
    )(ghq, y0)


def _rw_out_kernel(x_ref, g1_ref, y_ref, aux_ref, ln_ref, wo_ref, out_ref, z_scr, *, tm, tpb, ctx):
    i = pl.program_id(0)
    j = pl.program_id(1)

    @pl.when(j == 0)
    def _():
        bd = _head_bd()
        y = y_ref[0] + y_ref[1]
        mu = _head_sums(y, bd) * (1.0 / R_HEAD)
        yc = y - mu
        var = _head_sums(yc * yc, bd) * (1.0 / R_HEAD)
        z = yc * lax.rsqrt(var + R_LN_EPS) * ln_ref[0:1, :] + ln_ref[1:2, :]
        z_scr[...] = ((z + aux_ref[0].astype(F32)) * aux_ref[1].astype(F32)).astype(BF16)

    b = i // tpb
    is_ctx = ((i % tpb) * tm + _row_iota(tm)) < ctx
    out_ref[...] = x_ref[...] + _mod_rows(g1_ref, b, is_ctx) * _dot(z_scr[...], wo_ref[...])


def rw_out_block(x, mod, y, aux, ln_wb, w_o, *, lc, ctx, tm, tn=512):
    n, d = x.shape
    nj = d // tn
    return pl.pallas_call(
        functools.partial(_rw_out_kernel, tm=tm, tpb=lc // tm, ctx=ctx),
        out_shape=jax.ShapeDtypeStruct((n, d), F32),
        grid=(n // tm, nj),
        in_specs=[pl.BlockSpec((tm, tn), lambda i, j: (i, j)),
                  pl.BlockSpec((8, tn), lambda i, j: (0, 2 * nj + j)),
                  pl.BlockSpec((2, tm, d), lambda i, j: (0, i, 0)),
                  pl.BlockSpec((2, tm, d), lambda i, j: (0, i, 0)),
                  pl.BlockSpec((2, d), lambda i, j: (0, 0)),
                  pl.BlockSpec((d, tn), lambda i, j: (0, j))],
        out_specs=pl.BlockSpec((tm, tn), lambda i, j: (i, j)),
        scratch_shapes=[pltpu.VMEM((tm, d), BF16)],
        compiler_params=_cparams(("parallel", "arbitrary")),
        name="rw_out",
    )(x, mod, y, aux, ln_wb, w_o)


def odd_params(mu, w_r, w_k, w_v, w_o, w0, w1, w2, a0, a1, a2, g1, g2, k_k, k_a, r_k, ln_w, ln_b,
               v0=None, v1=None, v2=None):
    d = w_r.shape[0]
    dw, da, dg = w1.shape[-1], a1.shape[-1], g1.shape[-1]
    assert 2 * dw <= _LH_A - _LH_W and 2 * da <= _LH_G - _LH_A and dg <= _LH_V - _LH_G
    wl = jnp.zeros((d, _LH_END), F32)
    wl = wl.at[:, _LH_W:_LH_W + dw].set(w1[0]).at[:, _LH_W + dw:_LH_W + 2 * dw].set(w1[1])
    wl = wl.at[:, _LH_A:_LH_A + da].set(a1[0]).at[:, _LH_A + da:_LH_A + 2 * da].set(a1[1])
    wl = wl.at[:, _LH_G:_LH_G + dg].set(g1)
    w2p = jnp.zeros((2, 256, d), F32).at[0, :dw].set(w2[0]).at[1, dw:2 * dw].set(w2[1])
    a2p = jnp.zeros((2, 256, d), F32).at[0, :da].set(a2[0]).at[1, da:2 * da].set(a2[1])
    g2p = jnp.zeros((256, d), F32).at[:dg].set(g2)
    v2p = None
    vzero = jnp.zeros((d,), F32)
    if v1 is not None:
        dv = v1.shape[-1]
        assert dv <= _LH_END - _LH_V
        wl = wl.at[:, _LH_V:_LH_V + dv].set(v1)
        v2p = jnp.zeros((128, d), F32).at[:dv].set(v2).astype(BF16)
    pvec = jnp.stack([w0[0], w0[1], a0[0], a0[1], v0 if v0 is not None else vzero,
                      k_k, k_a, r_k.reshape(d)])
    return dict(mu=mu, w_lora1=wl.astype(BF16), w_rkv=jnp.stack([w_r, w_k, w_v]).astype(BF16),
                w2=w2p.astype(BF16), a2=a2p.astype(BF16), g2=g2p.astype(BF16), v2=v2p, pvec=pvec,
                ln_wb=jnp.stack([ln_w, ln_b]), w_o=w_o.astype(BF16))


def odd_layer(x, mod, nw, p, v_first, *, batch, lc, ctx, tm):
    rkv, lh = rw_project(x, mod, nw, p["mu"], p["w_lora1"], p["w_rkv"], lc=lc, ctx=ctx)
    t6, gt, v, aux = rw_gates(rkv, lh, v_first if p["v2"] is not None else None,
                              p["w2"], p["a2"], p["g2"], p["v2"], p["pvec"])
    ghq, y0 = rw_chunk_ops(t6, v, gt)
    y = rw_scan(ghq, y0, batch=batch, lc=lc, ctx=ctx)
    return rw_out_block(x, mod, y, aux, p["ln_wb"], p["w_o"], lc=lc, ctx=ctx, tm=tm), v


def _row_tile(lc, cap):
    return max(t for t in range(16, cap + 1, 16) if lc % t == 0)


def kernel(x, c, ctx, c_ctx, ada_w, ada_b, norm_mix, norm_ffn, ffn_w_up, ffn_conv_w, ffn_conv_b, ffn_w_down, norm_final, ev_w_in, ev_b_in, ev_w_out, s5_lam_re, s5_lam_im, s5_log_step, s5_b_re, s5_b_im, s5_c_re, s5_c_im, s5_d, s5_w_glu, s5_b_glu, ml_conv_w, ml_conv_b, ml_norm, rw_mu, rw_w_r, rw_w_k, rw_w_v, rw_w_o, rw_w0, rw_w1, rw_w2, rw_a0, rw_a1, rw_a2, rw_v0, rw_v1, rw_v2, rw_g1, rw_g2, rw_k_k, rw_k_a, rw_r_k, rw_ln_w, rw_ln_b):
    batch, seq, d = x.shape
    n_ctx = ctx.shape[1]
    lc = n_ctx + seq
    depth = ada_w.shape[0]
    n_even = ev_w_in.shape[0]
    assert batch < _CTX_ROW and n_ctx % 256 == 0 and seq % 256 == 0 and seq % GRID_W == 0
    tm = _row_tile(lc, 544)
    tm_small = _row_tile(lc, 272)

    xc = jnp.concatenate([ctx, x], axis=1).reshape(batch * lc, d)
    c8 = jnp.zeros((8, d), F32).at[:batch].set(c).at[_CTX_ROW].set(c_ctx)
    mods = adaln(c8, ada_w, ada_b)

    g, p_state = s5_lam_re.shape[-2:]
    flat = lambda a: a.reshape((n_even * 2,) + a.shape[2:])
    s5p = s5_params(flat(s5_lam_re), flat(s5_lam_im), s5_log_step.reshape(n_even * 2, g),
                    flat(s5_b_re), flat(s5_b_im), flat(s5_c_re), flat(s5_c_im))

    v_first = None
    for l in range(depth):
        j = l // 2
        if l % 2 == 0:
            ep = even_params(ev_w_in[j], ev_b_in[j], ev_w_out[j], s5_w_glu[j], s5_b_glu[j],
                             ml_conv_w[j], ml_conv_b[j], ml_norm[j])
            ops = s5_assemble(tuple(a[2 * j:2 * j + 2] for a in s5p), s5_d[j])
            xc = even_layer(xc, mods[l], norm_mix[l], ep, ops, batch=batch, lc=lc, ctx=n_ctx, tm=tm)
        else:
            extra = {} if j == 0 else dict(v0=rw_v0[j - 1], v1=rw_v1[j - 1], v2=rw_v2[j - 1])
            op = odd_params(rw_mu[j], rw_w_r[j], rw_w_k[j], rw_w_v[j], rw_w_o[j], rw_w0[j], rw_w1[j],
                            rw_w2[j], rw_a0[j], rw_a1[j], rw_a2[j], rw_g1[j], rw_g2[j], rw_k_k[j],
                            rw_k_a[j], rw_r_k[j], rw_ln_w[j], rw_ln_b[j], **extra)
            xc, v = odd_layer(xc, mods[l], norm_mix[l], op, v_first, batch=batch, lc=lc, ctx=n_ctx,
                              tm=tm_small)
            if j == 0:
                v_first = v
        xc = conv_ffn_block(xc, mods[l], norm_ffn[l], ffn_w_up[l].astype(BF16), ffn_conv_w[l],
                            ffn_conv_b[l], ffn_w_down[l].astype(BF16), lc=lc, ctx=n_ctx, tm=tm)
    out = final_norm(xc, norm_final, batch=batch, lc=lc, ctx=n_ctx)
    return out.reshape(batch, seq, d)
```

```python
import functools

import jax
import jax.numpy as jnp
from jax import lax
from jax.experimental import pallas as pl
from jax.experimental.pallas import tpu as pltpu

F32 = jnp.float32
BF16 = jnp.bfloat16
EPS = 1e-6
NEG_INF = -1e30
GRID_W = 64
S5_T = 16
M_HEADS = 4
M_CHUNK = 128
R_HEAD = 64
R_CHUNK = 64
R_LN_EPS = 64e-5
HIGHEST = lax.Precision.HIGHEST
VMEM_LIMIT = 56 * 1024 * 1024
_MXU_N = 256


def _cparams(sem):
    return pltpu.CompilerParams(dimension_semantics=sem, vmem_limit_bytes=VMEM_LIMIT)


def _dot(a, b):
    return jnp.dot(a, b, preferred_element_type=F32)


def _dot_nt(a, b):
    return lax.dot_general(a, b, (((1,), (1,)), ((), ())), preferred_element_type=F32)


def _dot_tn(a, b):
    return lax.dot_general(a, b, (((0,), (0,)), ((), ())), preferred_element_type=F32)


def _split3(x):
    h = x.astype(BF16)
    r = x - h.astype(F32)
    m = r.astype(BF16)
    l = (r - m.astype(F32)).astype(BF16)
    return h, m, l


def _dot_exact_lhs(a_bf16, x):
    h, m, l = _split3(x)
    return _dot(a_bf16, h) + _dot(a_bf16, m) + _dot(a_bf16, l)


def _dot_01_lhs(a_bf16, x):
    h = x.astype(BF16)
    l = (x - h.astype(F32)).astype(BF16)
    return _dot(a_bf16, h) + _dot(a_bf16, l)


_EXP_M05 = 0.6065306597126334


def _sigmoid(x):
    return jax.nn.sigmoid(x)


def _silu(x):
    return x * jax.nn.sigmoid(x)


def _gelu_tanh(x):
    return 0.5 * x * (1.0 + jnp.tanh(0.7978845608028654 * (x + 0.044715 * (x * x * x))))


def _row_iota(n):
    return lax.broadcasted_iota(jnp.int32, (n, 1), 0)


def _mod_rows(ref, b, is_ctx):
    lat = ref[pl.ds(b, 1), :]
    ctx = ref[pl.ds(_CTX_ROW, 1), :]
    return jnp.where(is_ctx, ctx, lat)


_CTX_ROW = 7


def _norm_mod(x, nw, sc, sh):
    ms = jnp.mean(x * x, axis=-1, keepdims=True)
    return x * lax.rsqrt(ms + EPS) * nw * (1.0 + sc) + sh


def _adaln_kernel(c_ref, w_ref, b_ref, o_ref):
    a = _silu(c_ref[...])
    o_ref[0] = _dot(a.astype(BF16), w_ref[0].astype(BF16)) + b_ref[0]


def adaln(c8, ada_w, ada_b, tn=1024):
    depth, d, n6 = ada_w.shape
    return pl.pallas_call(
        _adaln_kernel,
        out_shape=jax.ShapeDtypeStruct((depth, 8, n6), F32),
        grid=(depth, n6 // tn),
        in_specs=[pl.BlockSpec((8, d), lambda l, j: (0, 0)),
                  pl.BlockSpec((1, d, tn), lambda l, j: (l, 0, j)),
                  pl.BlockSpec((1, 1, tn), lambda l, j: (l, 0, j))],
        out_specs=pl.BlockSpec((1, 8, tn), lambda l, j: (l, 0, j)),
        compiler_params=_cparams(("parallel", "parallel")),
        name="adaln",
    )(c8, ada_w, ada_b.reshape(depth, 1, n6))


def _halo_specs(tm, d, n_rows, halo):
    r = tm // halo
    nblk = n_rows // halo
    return [pl.BlockSpec((tm, d), lambda i, j: (i, 0)),
            pl.BlockSpec((halo, d), lambda i, j: (jnp.maximum(i * r - 1, 0), 0)),
            pl.BlockSpec((halo, d), lambda i, j: (jnp.minimum(i * r + r, nblk - 1), 0))]


def _mod_spec(d, k):
    return pl.BlockSpec((8, d), lambda i, j: (0, k))


def _conv3(u, first, last, cw, cb):
    n = u.shape[0]
    up = jnp.where(first, 0.0, pltpu.roll(u, 1, 0))
    un = jnp.where(last, 0.0, pltpu.roll(u, n - 1, 0))
    return up * cw[0:1] + u * cw[1:2] + un * cw[2:3] + cb


def _ext_rows(xm_ref, xp_ref, xn_ref, sh_ref, sc_ref, nw_ref, i, tm, tpb, ctx, halo):
    b = i // tpb
    pos = (i % tpb) * tm - halo + _row_iota(tm + 2 * halo)
    is_ctx = pos < ctx
    xe = jnp.concatenate([xp_ref[...], xm_ref[...], xn_ref[...]], axis=0)
    h = _norm_mod(xe, nw_ref[...], _mod_rows(sc_ref, b, is_ctx), _mod_rows(sh_ref, b, is_ctx))
    return h, pos


def _ffn_kernel(xm_ref, xp_ref, xn_ref, sh_ref, sc_ref, g_ref, nw_ref, wa_ref, wg_ref,
                cwa_ref, cwg_ref, cba_ref, cbg_ref, wd_ref, xc_ref, o_ref, h_scr, act_scr,
                *, tm, tn, nj, tpb, ctx, lc):
    i = pl.program_id(0)
    j = pl.program_id(1)

    @pl.when(j == 0)
    def _():
        h, _ = _ext_rows(xm_ref, xp_ref, xn_ref, sh_ref, sc_ref, nw_ref, i, tm, tpb, ctx, 8)
        h_scr[...] = h.astype(BF16)

    @pl.when(j < nj)
    def _():
        pos = (i % tpb) * tm - 8 + _row_iota(tm + 16)
        first = (pos == 0) | (pos == ctx)
        last = (pos == ctx - 1) | (pos == lc - 1)
        h = h_scr[...]
        for c in range(tn // _MXU_N):
            cs = slice(c * _MXU_N, (c + 1) * _MXU_N)
            a = _conv3(_dot(h, wa_ref[:, cs]), first, last, cwa_ref[:, cs], cba_ref[:, cs])[8:8 + tm]
            g = _conv3(_dot(h, wg_ref[:, cs]), first, last, cwg_ref[:, cs], cbg_ref[:, cs])[8:8 + tm]
            col = pl.multiple_of(j * tn + c * _MXU_N, _MXU_N)
            act_scr[:, pl.ds(col, _MXU_N)] = (a * _silu(g)).astype(BF16)

    @pl.when(j >= nj)
    def _():
        b = i // tpb
        is_ctx = ((i % tpb) * tm + _row_iota(tm)) < ctx
        o_ref[...] = xc_ref[...] + _mod_rows(g_ref, b, is_ctx) * _dot(act_scr[...], wd_ref[...])


def conv_ffn_block(x, mod, nw, w_up, conv_w, conv_b, w_down, *, lc, ctx, tm, tn=512, tno=512):
    n, d = x.shape
    dff = w_down.shape[0]
    nj = dff // tn
    nk = d // tno
    up = lambda j: jnp.minimum(j, nj - 1)
    dn = lambda j: jnp.maximum(j - nj, 0)
    kern = functools.partial(_ffn_kernel, tm=tm, tn=tn, nj=nj, tpb=lc // tm, ctx=ctx, lc=lc)
    return pl.pallas_call(
        kern,
        out_shape=jax.ShapeDtypeStruct((n, d), F32),
        grid=(n // tm, nj + nk),
        in_specs=_halo_specs(tm, d, n, 8) + [
            _mod_spec(d, 3), _mod_spec(d, 4),
            pl.BlockSpec((8, tno), lambda i, j: (0, 5 * nk + dn(j))),
            pl.BlockSpec((1, d), lambda i, j: (0, 0)),
            pl.BlockSpec((d, tn), lambda i, j: (0, up(j))),
            pl.BlockSpec((d, tn), lambda i, j: (0, up(j) + nj)),
            pl.BlockSpec((3, tn), lambda i, j: (0, up(j))),
            pl.BlockSpec((3, tn), lambda i, j: (0, up(j) + nj)),
            pl.BlockSpec((1, tn), lambda i, j: (0, up(j))),
            pl.BlockSpec((1, tn), lambda i, j: (0, up(j) + nj)),
            pl.BlockSpec((dff, tno), lambda i, j: (0, dn(j))),
            pl.BlockSpec((tm, tno), lambda i, j: (i, dn(j))),
        ],
        out_specs=pl.BlockSpec((tm, tno), lambda i, j: (i, dn(j))),
        scratch_shapes=[pltpu.VMEM((tm + 16, d), BF16), pltpu.VMEM((tm, dff), BF16)],
        compiler_params=_cparams(("parallel", "arbitrary")),
        name="conv_ffn",
    )(x, x, x, mod, mod, mod, nw.reshape(1, d), w_up, w_up, conv_w, conv_w,
      conv_b.reshape(1, -1), conv_b.reshape(1, -1), w_down, x)


def _final_norm_kernel(x_ref, w_ref, o_ref):
    x = x_ref[...]
    o_ref[...] = x * lax.rsqrt(jnp.mean(x * x, axis=-1, keepdims=True) + EPS) * w_ref[...]


def final_norm(x, w, *, batch, lc, ctx, tm=256):
    n, d = x.shape
    tpb = lc // tm
    skip = ctx // tm
    per = tpb - skip
    return pl.pallas_call(
        _final_norm_kernel,
        out_shape=jax.ShapeDtypeStruct((batch * per * tm, d), F32),
        grid=(batch, per),
        in_specs=[pl.BlockSpec((tm, d), lambda b, t: (b * tpb + skip + t, 0)),
                  pl.BlockSpec((1, d), lambda b, t: (0, 0))],
        out_specs=pl.BlockSpec((tm, d), lambda b, t: (b * per + t, 0)),
        compiler_params=_cparams(("parallel", "parallel")),
        name="final_norm",
    )(x, w.reshape(1, d))


def _normproj_kernel(*refs, tm, tpb, ctx, lc, conv):
    if conv:
        (xm_ref, xp_ref, xn_ref, sh_ref, sc_ref, nw_ref, w_ref, b_ref, cw_ref, cb_ref, s_ref,
         o_ref, h_scr) = refs
    else:
        xm_ref, sh_ref, sc_ref, nw_ref, w_ref, b_ref, o_ref, h_scr = refs
    i = pl.program_id(0)
    j = pl.program_id(1)

    @pl.when(j == 0)
    def _():
        if conv:
            h, _ = _ext_rows(xm_ref, xp_ref, xn_ref, sh_ref, sc_ref, nw_ref, i, tm, tpb, ctx, 8)
        else:
            b = i // tpb
            is_ctx = ((i % tpb) * tm + _row_iota(tm)) < ctx
            h = _norm_mod(xm_ref[...], nw_ref[...], _mod_rows(sc_ref, b, is_ctx),
                          _mod_rows(sh_ref, b, is_ctx))
        h_scr[...] = h.astype(BF16)

    u = _dot(h_scr[...], w_ref[...]) + b_ref[...]
    if conv:
        pos = (i % tpb) * tm - 8 + _row_iota(tm + 16)
        first = (pos == 0) | (pos == ctx)
        last = (pos == ctx - 1) | (pos == lc - 1)
        u = _silu(_conv3(u, first, last, cw_ref[...], cb_ref[...])[8:8 + tm]) * s_ref[...]
    o_ref[...] = u.astype(o_ref.dtype)


def norm_proj(x, mod, nw, w, bias, *, lc, ctx, tm, tn, out_dtype, conv=None):
    n, d = x.shape
    ncol = w.shape[1]
    kern = functools.partial(_normproj_kernel, tm=tm, tpb=lc // tm, ctx=ctx, lc=lc,
                             conv=conv is not None)
    col = lambda r: pl.BlockSpec((r, tn), lambda i, j: (0, j))
    if conv is not None:
        cw, cb, scale = conv
        in_specs = _halo_specs(tm, d, n, 8) + [
            _mod_spec(d, 0), _mod_spec(d, 1), pl.BlockSpec((1, d), lambda i, j: (0, 0)),
            pl.BlockSpec((d, tn), lambda i, j: (0, j)), col(1), col(3), col(1), col(1)]
        args = (x, x, x, mod, mod, nw.reshape(1, d), w, bias.reshape(1, ncol), cw,
                cb.reshape(1, ncol), scale.reshape(1, ncol))
        rows = tm + 16
    else:
        in_specs = [pl.BlockSpec((tm, d), lambda i, j: (i, 0)),
                    _mod_spec(d, 0), _mod_spec(d, 1), pl.BlockSpec((1, d), lambda i, j: (0, 0)),
                    pl.BlockSpec((d, tn), lambda i, j: (0, j)), col(1)]
        args = (x, mod, mod, nw.reshape(1, d), w, bias.reshape(1, ncol))
        rows = tm
    return pl.pallas_call(
        kern,
        out_shape=jax.ShapeDtypeStruct((n, ncol), out_dtype),
        grid=(n // tm, ncol // tn),
        in_specs=in_specs,
        out_specs=pl.BlockSpec((tm, tn), lambda i, j: (i, j)),
        scratch_shapes=[pltpu.VMEM((rows, d), BF16)],
        compiler_params=_cparams(("parallel", "arbitrary")),
        name="norm_proj_conv" if conv is not None else "norm_proj",
    )(*args)


def _s5_param_kernel(lr_ref, li_ref, ls_ref, bbr_ref, bbi_ref, cr_ref, ci_ref,
                     wre_ref, wim_ref, ere_ref, eim_ref, k_ref, lam_re_ref, lam_im_ref, *, t, ch):
    step = jnp.exp(ls_ref[0])
    lr = lr_ref[0]
    li = li_ref[0]
    p = lr.shape[-1]
    sr = lr * step
    si = li * step

    def power(tau):
        mag = jnp.exp(tau * sr)
        return mag * jnp.cos(tau * si), mag * jnp.sin(tau * si)

    ab_re, ab_im = power(1.0)
    den = lr * lr + li * li
    co_re = ((ab_re - 1.0) * lr + ab_im * li) / den
    co_im = (ab_im * lr - (ab_re - 1.0) * li) / den
    b_re = bbr_ref[0]
    b_im = bbi_ref[0]
    bb_re = co_re * b_re - co_im * b_im
    bb_im = co_re * b_im + co_im * b_re
    tile = lambda m: jnp.concatenate([m] * t, axis=0)
    tau = (lax.broadcasted_iota(jnp.int32, (t * ch, p), 0) // ch).astype(F32)
    pr, pi = power(tau)
    bbr_t, bbi_t = tile(bb_re), tile(bb_im)
    w_re = pr * bbr_t - pi * bbi_t
    w_im = pr * bbi_t + pi * bbr_t
    wre_ref[0] = w_re
    wim_ref[0] = w_im
    qr, qi = power(tau + 1.0)
    c_re = cr_ref[0]
    c_im = ci_ref[0]
    cr_t, ci_t = tile(c_re), tile(c_im)
    ere_ref[0] = cr_t * qr - ci_t * qi
    eim_ref[0] = -(cr_t * qi + ci_t * qr)
    k_ref[0] = (lax.dot_general(c_re, w_re, (((1,), (1,)), ((), ())), precision=HIGHEST,
                                preferred_element_type=F32)
                - lax.dot_general(c_im, w_im, (((1,), (1,)), ((), ())), precision=HIGHEST,
                                  preferred_element_type=F32))
    lt_re, lt_im = power(float(t))
    lam_re_ref[0] = lt_re
    lam_im_ref[0] = lt_im


def s5_params(lam_re, lam_im, log_step, b_re, b_im, c_re, c_im, t=S5_T):
    nd, g, p = lam_re.shape
    ch = c_re.shape[2]
    m = nd * g
    r3 = lambda a: a.reshape(m, 1, p)
    bt = lambda a: jnp.swapaxes(a, -1, -2).reshape(m, ch, p)
    vec = pl.BlockSpec((1, 1, p), lambda i: (i, 0, 0))
    mat = pl.BlockSpec((1, ch, p), lambda i: (i, 0, 0))
    big = pl.BlockSpec((1, t * ch, p), lambda i: (i, 0, 0))
    outs = pl.pallas_call(
        functools.partial(_s5_param_kernel, t=t, ch=ch),
        out_shape=[jax.ShapeDtypeStruct((m, t * ch, p), F32)] * 4
        + [jax.ShapeDtypeStruct((m, ch, t * ch), F32)] + [jax.ShapeDtypeStruct((m, 1, p), F32)] * 2,
        grid=(m,),
        in_specs=[vec, vec, pl.BlockSpec((1, 1, 1), lambda i: (i, 0, 0)), mat, mat, mat, mat],
        out_specs=[big] * 4 + [pl.BlockSpec((1, ch, t * ch), lambda i: (i, 0, 0)), vec, vec],
        compiler_params=_cparams(("parallel",)),
        name="s5_params",
    )(r3(lam_re), r3(lam_im), log_step.reshape(m, 1, 1), bt(b_re), bt(b_im),
      c_re.reshape(m, ch, p), c_im.reshape(m, ch, p))
    w_re, w_im, e_re, e_im, k, l_re, l_im = outs
    sh = lambda a: a.reshape(nd, g, t, ch, p)
    return (sh(w_re), sh(w_im), sh(e_re), sh(e_im), k.reshape(nd, g, ch, t, ch),
            l_re.reshape(nd, g, p), l_im.reshape(nd, g, p))


def s5_assemble(params, d_skip, t=S5_T):
    w_re, w_im, e_re, e_im, k, l_re, l_im = params
    nd, g, _, ch, p = w_re.shape
    tc = t * ch
    s_i = jnp.arange(t)[:, None]
    t_i = jnp.arange(t)[None, :]
    kf = jnp.take(k[0], jnp.clip(t_i - s_i, 0, t - 1), axis=2)
    kb = jnp.take(k[1], jnp.clip(s_i - t_i, 0, t - 1), axis=2)
    mf = jnp.where((t_i >= s_i)[None, None, :, :, None], kf, 0.0)
    mb = jnp.where((s_i >= t_i)[None, None, :, :, None], kb, 0.0)
    m = jnp.transpose(mf + mb, (0, 2, 4, 3, 1))
    eye = (jnp.eye(t)[:, None, :, None] * jnp.eye(ch)[None, :, None, :])
    m = m + eye[None] * d_skip.reshape(g, 1, 1, 1, ch)
    m = m.reshape(g, tc, tc)
    def fmat(wr, wi):
        wr = wr.reshape(g, tc, p)
        wi = wi.reshape(g, tc, p)
        return jnp.concatenate([wr, wi, wi, wr], axis=-1)
    f2 = jnp.concatenate([fmat(w_re[0][:, ::-1], w_im[0][:, ::-1]), fmat(w_re[1], w_im[1])], axis=-1)
    def emat(er, ei):
        return jnp.concatenate([jnp.swapaxes(er.reshape(g, tc, p), 1, 2),
                                jnp.swapaxes(ei.reshape(g, tc, p), 1, 2)], axis=1)
    e2 = jnp.concatenate([emat(e_re[0], e_im[0]), emat(e_re[1][:, ::-1], e_im[1][:, ::-1])], axis=1)
    def coef(lr, li):
        a = jnp.concatenate([lr, lr], axis=-1)
        b1 = jnp.concatenate([-li, li], axis=-1)
        b2 = jnp.concatenate([li, -li], axis=-1)
        return jnp.concatenate([a, a], axis=-1), jnp.concatenate([b1, b2], axis=-1)
    af, bf = coef(l_re[0], l_im[0])
    ab, bb = coef(l_re[1], l_im[1])
    coefs = jnp.stack([jnp.concatenate([af, ab], axis=-1), jnp.concatenate([bf, bb], axis=-1)], axis=1)
    return m.astype(BF16), f2.astype(BF16), e2.astype(BF16), coefs


def _s5_in_kernel(u_ref, f_ref, z_ref):
    z_ref[...] = _dot(u_ref[0], f_ref[0])


def s5_chunk_inputs(ut, f2):
    g, r, tc = ut.shape
    w = f2.shape[-1]
    return pl.pallas_call(
        _s5_in_kernel,
        out_shape=jax.ShapeDtypeStruct((r, g * w), F32),
        grid=(g,),
        in_specs=[pl.BlockSpec((1, r, tc), lambda i: (i, 0, 0)),
                  pl.BlockSpec((1, tc, w), lambda i: (i, 0, 0))],
        out_specs=pl.BlockSpec((r, w), lambda i: (0, i)),
        compiler_params=_cparams(("parallel",)),
        name="s5_chunk_inputs",
    )(ut, f2)


def _s5_scan_kernel(z_ref, c_ref, h_ref, *, gb, nc, nc_ctx, lw):
    nb = z_ref.shape[0]
    ng, ng_ctx = nc // 8, nc_ctx // 8

    def body(jg, carry):
        rows = (pl.multiple_of(jg * 8, 8),
                pl.multiple_of(_chunk_order(1, jg, ng, ng_ctx) * 8, 8))
        new = list(carry)
        for gi in range(gb):
            for di in range(2):
                idx = 2 * (2 * gi + di)
                hs, hx = new[idx], new[idx + 1]
                base = (gi * 4 + 2 * di) * lw
                z8 = z_ref[:, pl.ds(rows[di], 8), base:base + lw]
                zx8 = z_ref[:, pl.ds(rows[di], 8), base + lw:base + 2 * lw]
                a = c_ref[gi, 0:1, 2 * di * lw:(2 * di + 1) * lw]
                b1 = c_ref[gi, 1:2, 2 * di * lw:(2 * di + 1) * lw]
                b2 = c_ref[gi, 1:2, (2 * di + 1) * lw:(2 * di + 2) * lw]
                entry = [None] * 8
                for s in range(8):
                    r = s if di == 0 else 7 - s
                    entry[r] = hs
                    hs, hx = (a * hs + b1 * hx + z8[:, r:r + 1, :],
                              a * hx + b2 * hs + zx8[:, r:r + 1, :])
                new[idx], new[idx + 1] = hs, hx
                h_ref[:, pl.ds(rows[di], 8), (gi * 2 + di) * lw:(gi * 2 + di + 1) * lw] = (
                    jnp.concatenate(entry, axis=1))
        return tuple(new)

    init = tuple(jnp.zeros((nb, 1, lw), F32) for _ in range(4 * gb))
    lax.fori_loop(0, ng, body, init)


def s5_chunk_scan(z, coefs, *, batch, nc, nc_ctx, gb=2):
    r, wtot = z.shape
    g = coefs.shape[0]
    lw = wtot // g // 4
    assert nc % 8 == 0 and nc_ctx % 8 == 0
    z3 = z.reshape(batch, nc, wtot)
    return pl.pallas_call(
        functools.partial(_s5_scan_kernel, gb=gb, nc=nc, nc_ctx=nc_ctx, lw=lw),
        out_shape=jax.ShapeDtypeStruct((batch, nc, g * 2 * lw), F32),
        grid=(g // gb,),
        in_specs=[pl.BlockSpec((batch, nc, gb * 4 * lw), lambda i: (0, 0, i)),
                  pl.BlockSpec((gb, 2, 4 * lw), lambda i: (i, 0, 0))],
        out_specs=pl.BlockSpec((batch, nc, gb * 2 * lw), lambda i: (0, 0, i)),
        compiler_params=_cparams(("parallel",)),
        name="s5_chunk_scan",
    )(z3, coefs).reshape(r, g * 2 * lw)


def _s5_out_kernel(u_ref, m_ref, h_ref, e_ref, y_ref):
    y = _dot(u_ref[0], m_ref[0]) + _dot(h_ref[...].astype(BF16), e_ref[0])
    y_ref[0] = y.astype(y_ref.dtype)


def s5_chunk_outputs(ut, m, hs, e2):
    g, r, tc = ut.shape
    hw = e2.shape[1]
    return pl.pallas_call(
        _s5_out_kernel,
        out_shape=jax.ShapeDtypeStruct((g, r, tc), BF16),
        grid=(g,),
        in_specs=[pl.BlockSpec((1, r, tc), lambda i: (i, 0, 0)),
                  pl.BlockSpec((1, tc, tc), lambda i: (i, 0, 0)),
                  pl.BlockSpec((r, hw), lambda i: (0, i)),
                  pl.BlockSpec((1, hw, tc), lambda i: (i, 0, 0))],
        out_specs=pl.BlockSpec((1, r, tc), lambda i: (i, 0, 0)),
        compiler_params=_cparams(("parallel",)),
        name="s5_chunk_outputs",
    )(ut, m, hs, e2)


def s5_mix(u, ops, *, batch, lc, ctx, t=S5_T):
    m, f2, e2, coefs = ops
    n, w = u.shape
    g = m.shape[0]
    ch = w // g
    r = n // t
    ut = jnp.transpose(u.reshape(r, t, g, ch), (2, 0, 1, 3)).reshape(g, r, t * ch)
    z = s5_chunk_inputs(ut, f2)
    hs = s5_chunk_scan(z, coefs, batch=batch, nc=lc // t, nc_ctx=ctx // t)
    yt = s5_chunk_outputs(ut, m, hs, e2)
    return jnp.transpose(yt.reshape(g, r, t, ch), (1, 2, 0, 3)).reshape(n, w)


def _chunk_order(d, j, nc, nc_ctx):
    bwd = jnp.where(j < nc_ctx, nc_ctx - 1 - j, nc - 1 - (j - nc_ctx))
    return jnp.where(d == 0, j, bwd)


def _mlstm_kernel(qkf_ref, vf_ref, gf_ref, qkb_ref, vb_ref, gb_ref, of_ref, ob_ref,
                  c_scr, n_scr, m_scr, *, nh):
    j = pl.program_id(1)
    tc = qkf_ref.shape[0]
    mw = vf_ref.shape[1]
    dh = mw // nh

    @pl.when(j == 0)
    def _():
        c_scr[...] = jnp.zeros_like(c_scr)
        n_scr[...] = jnp.zeros_like(n_scr)
        m_scr[...] = jnp.full_like(m_scr, NEG_INF)

    row = lax.broadcasted_iota(jnp.int32, (tc, tc), 0)
    col = lax.broadcasted_iota(jnp.int32, (tc, tc), 1)
    lane = lax.broadcasted_iota(jnp.int32, (tc, 128), 1)
    chains = [(d, hd) for d in range(2) for hd in range(nh)]
    each = lambda f, *cols: [f(*xs) for xs in zip(*cols)]
    qk_refs, v_refs, g_vals = (qkf_ref, qkb_ref), (vf_ref, vb_ref), (gf_ref[...], gb_ref[...])
    vis_d = [col <= row, col >= row]
    tri_d = [jnp.where(m, 1.0, 0.0).astype(BF16) for m in vis_d]
    pick = lambda d, c: jnp.sum(jnp.where(lane == c, g_vals[d], 0.0), axis=1, keepdims=True)
    ic = [pick(d, d * 2 * nh + hd) for d, hd in chains]
    fc = [pick(d, d * 2 * nh + nh + hd) for d, hd in chains]
    lf = each(lambda f: jnp.minimum(f, 0.0) - jnp.log(1.0 + jnp.exp(-jnp.abs(f))), fc)
    b1 = []
    for d in range(2):
        cat = jnp.concatenate([jnp.broadcast_to(lf[d * nh + hd], (tc, tc)) for hd in range(nh)], axis=1)
        cum = _dot_exact_lhs(tri_d[d], cat)
        b1 += [cum[:, hd * tc:(hd + 1) * tc] for hd in range(nh)]
    b2 = each(lambda m: m.T, b1)
    ic2 = each(lambda c: jnp.broadcast_to(c, (tc, tc)).T, ic)
    total = each(lambda f: jnp.sum(f, axis=0, keepdims=True), lf)
    m_prev = [m_scr[c, 0:1, 0:1] for c in range(len(chains))]
    logw = [jnp.where(vis_d[d], b1[c] - b2[c] + ic2[c], NEG_INF) for c, (d, _) in enumerate(chains)]
    bcol = each(lambda m: m[:, 0:1], b1)
    inter = each(lambda b, m: b + m, bcol, m_prev)
    m_row = each(lambda lw, it: jnp.maximum(jnp.max(lw, axis=1, keepdims=True), it), logw, inter)
    q = [qk_refs[d][:, hd * dh:(hd + 1) * dh] for d, hd in chains]
    k = [qk_refs[d][:, mw + hd * dh:mw + (hd + 1) * dh] for d, hd in chains]
    v = [v_refs[d][:, hd * dh:(hd + 1) * dh] for d, hd in chains]
    s = each(lambda qq, kk, lw, mr: _dot_nt(qq, kk) * jnp.exp(lw - mr), q, k, logw, m_row)
    w_inter = each(lambda it, mr: jnp.exp(it - mr), inter, m_row)
    c_old = [c_scr[c] for c in range(len(chains))]
    n_old = [n_scr[c, 0:1, :] for c in range(len(chains))]
    num = each(lambda ss, vv, wi, qq, co: _dot(ss.astype(BF16), vv) + wi * _dot(qq, co.astype(BF16)),
               s, v, w_inter, q, c_old)
    den = each(lambda ss, wi, qq, no: jnp.sum(ss, axis=1, keepdims=True)
               + wi * jnp.sum(qq.astype(F32) * no, axis=1, keepdims=True), s, w_inter, q, n_old)
    h = each(lambda nu, de, mr: nu / jnp.maximum(jnp.abs(de), jnp.exp(-mr)), num, den, m_row)
    of_ref[...] = jnp.concatenate(h[:nh], axis=1).astype(of_ref.dtype)
    ob_ref[...] = jnp.concatenate(h[nh:], axis=1).astype(ob_ref.dtype)
    lws = each(lambda t, b, c: t - b + c, total, bcol, ic)
    m_new = each(lambda t, m, l: jnp.maximum(t + m, jnp.max(l, axis=0, keepdims=True)), total, m_prev, lws)
    ek = each(lambda l, m, kk: jnp.exp(l - m) * kk.astype(F32), lws, m_new, k)
    cw = each(lambda t, m, mn: jnp.exp(t + m - mn), total, m_prev, m_new)
    c_new = each(lambda w, co, e, vv: w * co + _dot_tn(e.astype(BF16), vv), cw, c_old, ek, v)
    for c in range(len(chains)):
        c_scr[c] = c_new[c]
        n_scr[c, 0:1, :] = cw[c] * n_old[c] + jnp.sum(ek[c], axis=0, keepdims=True)
        m_scr[c] = jnp.broadcast_to(m_new[c], m_scr.shape[1:])


def mlstm_mix(qk, uvo, gates, *, batch, lc, ctx, nh=M_HEADS, tc=M_CHUNK):
    n = qk.shape[0]
    mw = qk.shape[1] // 2
    dh = mw // nh
    nc = lc // tc
    nc_ctx = ctx // tc
    assert uvo.shape[1] == 3 * mw
    rb = lambda d: (lambda b, j: b * nc + _chunk_order(d, j, nc, nc_ctx))
    ins, args = [], []
    for d in range(2):
        ins += [pl.BlockSpec((tc, 2 * mw), lambda b, j, r=rb(d): (r(b, j), 0)),
                pl.BlockSpec((tc, mw), lambda b, j, r=rb(d): (r(b, j), 1)),
                pl.BlockSpec((tc, 128), lambda b, j, r=rb(d): (r(b, j), 0))]
        args += [qk, uvo, gates]
    return pl.pallas_call(
        functools.partial(_mlstm_kernel, nh=nh),
        out_shape=[jax.ShapeDtypeStruct((n, mw), F32)] * 2,
        grid=(batch, nc),
        in_specs=ins,
        out_specs=[pl.BlockSpec((tc, mw), lambda b, j, r=rb(d): (r(b, j), 0)) for d in range(2)],
        scratch_shapes=[pltpu.VMEM((2 * nh, dh, dh), F32), pltpu.VMEM((2 * nh, 8, dh), F32),
                        pltpu.VMEM((2 * nh, 8, 128), F32)],
        compiler_params=_cparams(("parallel", "arbitrary")),
        name="mlstm",
    )(*args)


def _even_out_kernel(x_ref, g1_ref, y_ref, hf_ref, hb_ref, o_ref, mn_ref, wg_ref, bg_ref, wo_ref,
                     out_ref, mix_scr, *, tm, tpb, ctx, nh):
    i = pl.program_id(0)
    j = pl.program_id(1)

    @pl.when(j == 0)
    def _():
        s = _gelu_tanh(y_ref[...].astype(F32))
        glu = s * _sigmoid(_dot(s.astype(BF16), wg_ref[...]) + bg_ref[...])
        hm = hf_ref[...] + hb_ref[...]
        dh = hm.shape[1] // nh
        parts = []
        for hd in range(nh):
            seg = hm[:, hd * dh:(hd + 1) * dh]
            parts.append(seg * lax.rsqrt(jnp.mean(seg * seg, axis=-1, keepdims=True) + EPS))
        ml = jnp.concatenate(parts, axis=1) * mn_ref[...] * _sigmoid(o_ref[...].astype(F32))
        mix_scr[...] = jnp.concatenate([glu, ml], axis=1).astype(BF16)

    b = i // tpb
    is_ctx = ((i % tpb) * tm + _row_iota(tm)) < ctx
    out_ref[...] = x_ref[...] + _mod_rows(g1_ref, b, is_ctx) * _dot(mix_scr[...], wo_ref[...])


def even_out_block(x, mod, ys5, hf, hb, uvo, ml_norm, w_glu, b_glu, w_out, *, lc, ctx, tm, tn=512, nh=M_HEADS):
    n, d = x.shape
    sw = ys5.shape[1]
    mw = hf.shape[1]
    assert sw == mw
    nj = d // tn
    return pl.pallas_call(
        functools.partial(_even_out_kernel, tm=tm, tpb=lc // tm, ctx=ctx, nh=nh),
        out_shape=jax.ShapeDtypeStruct((n, d), F32),
        grid=(n // tm, nj),
        in_specs=[pl.BlockSpec((tm, tn), lambda i, j: (i, j)),
                  pl.BlockSpec((8, tn), lambda i, j: (0, 2 * nj + j)),
                  pl.BlockSpec((tm, sw), lambda i, j: (i, 0)),
                  pl.BlockSpec((tm, mw), lambda i, j: (i, 0)),
                  pl.BlockSpec((tm, mw), lambda i, j: (i, 0)),
                  pl.BlockSpec((tm, mw), lambda i, j: (i, 2)),
                  pl.BlockSpec((1, mw), lambda i, j: (0, 0)),
                  pl.BlockSpec((sw, sw), lambda i, j: (0, 0)),
                  pl.BlockSpec((1, sw), lambda i, j: (0, 0)),
                  pl.BlockSpec((sw + mw, tn), lambda i, j: (0, j))],
        out_specs=pl.BlockSpec((tm, tn), lambda i, j: (i, j)),
        scratch_shapes=[pltpu.VMEM((tm, sw + mw), BF16)],
        compiler_params=_cparams(("parallel", "arbitrary")),
        name="even_out",
    )(x, mod, ys5, hf, hb, uvo, ml_norm.reshape(1, mw), w_glu, b_glu.reshape(1, sw), w_out)


def even_layer(x, mod, nw, p, s5_ops, *, batch, lc, ctx, tm):
    qk = norm_proj(x, mod, nw, p["w_qk"], p["b_qk"], lc=lc, ctx=ctx, tm=tm, tn=512, out_dtype=BF16,
                   conv=(p["conv_w"], p["conv_b"], p["qk_scale"]))
    uvo = norm_proj(x, mod, nw, p["w_uvo"], p["b_uvo"], lc=lc, ctx=ctx, tm=tm, tn=512, out_dtype=BF16)
    gates = norm_proj(x, mod, nw, p["w_gate"], p["b_gate"], lc=lc, ctx=ctx, tm=tm, tn=128, out_dtype=F32)
    sw = p["w_glu"].shape[0]
    ys5 = s5_mix(uvo[:, :sw], s5_ops, batch=batch, lc=lc, ctx=ctx)
    hf, hb = mlstm_mix(qk, uvo, gates, batch=batch, lc=lc, ctx=ctx)
    return even_out_block(x, mod, ys5, hf, hb, uvo, p["ml_norm"], p["w_glu"], p["b_glu"], p["w_out"],
                          lc=lc, ctx=ctx, tm=tm)


def even_params(w_in, b_in, w_out, w_glu, b_glu, conv_w, conv_b, ml_norm, nh=M_HEADS):
    sw = w_glu.shape[0]
    mw = ml_norm.shape[0]
    c0, c1, c2 = sw, sw + 2 * mw, sw + 4 * mw
    ng = w_in.shape[1] - c2
    scale = jnp.concatenate([jnp.ones((mw,), F32), jnp.full((mw,), (mw // nh) ** -0.5, F32)])
    uvo_cols = lambda a: jnp.concatenate([a[..., :c0], a[..., c1:c2]], axis=-1)
    return dict(
        w_qk=w_in[:, c0:c1].astype(BF16), b_qk=b_in[c0:c1], qk_scale=scale,
        w_uvo=uvo_cols(w_in).astype(BF16), b_uvo=uvo_cols(b_in),
        w_gate=jnp.pad(w_in[:, c2:], ((0, 0), (0, 128 - ng))).astype(BF16),
        b_gate=jnp.pad(b_in[c2:], (0, 128 - ng)),
        conv_w=conv_w, conv_b=conv_b, ml_norm=ml_norm,
        w_glu=w_glu.astype(BF16), b_glu=b_glu, w_out=w_out.astype(BF16))


_LH_W, _LH_A, _LH_G, _LH_V, _LH_END = 0, 256, 512, 768, 896


def _rw_proj_kernel(xm_ref, xp_ref, xn_ref, sh_ref, sc_ref, nw_ref, mu_ref, wl_ref, w_ref,
                    rkv_ref, lh_ref, mix_scr, *, tm, tn, tpb, ctx, lc, nb):
    i = pl.program_id(0)
    j = pl.program_id(1)
    halo = GRID_W

    @pl.when(j == 0)
    def _():
        he, _ = _ext_rows(xm_ref, xp_ref, xn_ref, sh_ref, sc_ref, nw_ref, i, tm, tpb, ctx, halo)
        n = tm + 2 * halo
        d = he.shape[1]
        q = d // 4
        h = he[halo:halo + tm]
        hprev = pltpu.roll(he, 1, 0)[halo:halo + tm]
        hnext = pltpu.roll(he, n - 1, 0)[halo:halo + tm]
        hup = he[0:tm]
        hdown = he[2 * halo:2 * halo + tm]
        pos = (i % tpb) * tm + _row_iota(tm)
        is_ctx = pos < ctx
        pl_ = pos - ctx
        gcol = pl_ & (GRID_W - 1)
        ok_prev = jnp.where(is_ctx, pos, gcol) != 0
        ok_q1 = jnp.where(is_ctx, pos, gcol - (GRID_W - 1)) != 0
        ok_q2 = jnp.where(is_ctx, pos - (ctx - 1), jnp.maximum(pl_ - (GRID_W - 1), 0)) != 0
        ok_q3 = jnp.where(is_ctx, pos - (ctx - 1), jnp.maximum((lc - ctx) - GRID_W - pl_, 0)) != 0
        s0 = jnp.where(ok_prev, hprev[:, :q], 0.0)
        s1 = jnp.where(ok_q1, jnp.where(is_ctx, hprev[:, q:2 * q], hnext[:, q:2 * q]), 0.0)
        s2 = jnp.where(ok_q2, jnp.where(is_ctx, hnext[:, 2 * q:3 * q], hup[:, 2 * q:3 * q]), 0.0)
        s3 = jnp.where(ok_q3, jnp.where(is_ctx, hnext[:, 3 * q:], hdown[:, 3 * q:]), 0.0)
        xx = jnp.concatenate([s0, s1, s2, s3], axis=1) - h
        mix = lambda r: (h + xx * mu_ref[r:r + 1, :]).astype(BF16)
        xv = mix(3)
        mix_scr[0] = mix(0)
        mix_scr[1] = mix(2)
        mix_scr[2] = xv
        lh_ref[:, _LH_W:_LH_A] = jnp.tanh(_dot(mix(1), wl_ref[:, _LH_W:_LH_A]))
        lh_ref[:, _LH_A:_LH_G] = _dot(mix(4), wl_ref[:, _LH_A:_LH_G])
        lh_ref[:, _LH_G:_LH_V] = _sigmoid(_dot(mix(5), wl_ref[:, _LH_G:_LH_V]))
        lh_ref[:, _LH_V:_LH_END] = _dot(xv, wl_ref[:, _LH_V:_LH_END])

    cols = pl.ds(pl.multiple_of((j % nb) * tn, tn), tn)
    rkv_ref[...] = _dot(mix_scr[j // nb], w_ref[j // nb, :, cols])


def rw_project(x, mod, nw, mu, w_lora1, w_rkv, *, lc, ctx, tm=256, tn=512):
    n, d = x.shape
    nb = d // tn
    return pl.pallas_call(
        functools.partial(_rw_proj_kernel, tm=tm, tn=tn, tpb=lc // tm, ctx=ctx, lc=lc, nb=nb),
        out_shape=[jax.ShapeDtypeStruct((3, n, d), F32), jax.ShapeDtypeStruct((n, _LH_END), F32)],
        grid=(n // tm, 3 * nb),
        in_specs=_halo_specs(tm, d, n, GRID_W) + [
            _mod_spec(d, 0), _mod_spec(d, 1), pl.BlockSpec((1, d), lambda i, j: (0, 0)),
            pl.BlockSpec((6, d), lambda i, j: (0, 0)),
            pl.BlockSpec((d, _LH_END), lambda i, j: (0, 0)),
            pl.BlockSpec((3, d, d), lambda i, j: (0, 0, 0))],
        out_specs=[pl.BlockSpec((None, tm, tn), lambda i, j: (j // nb, i, j % nb)),
                   pl.BlockSpec((tm, _LH_END), lambda i, j: (i, 0))],
        scratch_shapes=[pltpu.VMEM((3, tm, d), BF16)],
        compiler_params=_cparams(("parallel", "arbitrary")),
        name="rw_project",
    )(x, x, x, mod, mod, nw.reshape(1, d), mu, w_lora1, w_rkv)


def _head_sums(x, bd):
    hi = x.astype(BF16)
    lo = (x - hi.astype(F32)).astype(BF16)
    parts = []
    for c in range(x.shape[1] // 128):
        sl = slice(c * 128, (c + 1) * 128)
        parts.append(_dot(hi[:, sl], bd) + _dot(lo[:, sl], bd))
    return jnp.concatenate(parts, axis=1)


def _head_bd():
    r = lax.broadcasted_iota(jnp.int32, (128, 128), 0) // R_HEAD
    c = lax.broadcasted_iota(jnp.int32, (128, 128), 1) // R_HEAD
    return jnp.where(r == c, 1.0, 0.0).astype(BF16)


def _rw_gate_kernel(*refs, tm, has_vfirst):
    if has_vfirst:
        (rkv_ref, lh_ref, vf_ref, w2_ref, a2_ref, g2_ref, v2_ref, pv_ref,
         t6_ref, gt_ref, v_ref, aux_ref) = refs
    else:
        rkv_ref, lh_ref, w2_ref, a2_ref, g2_ref, pv_ref, t6_ref, gt_ref, v_ref, aux_ref = refs
    r = rkv_ref[0]
    k = rkv_ref[1]
    v = rkv_ref[2]
    pv = pv_ref[...]
    seg = lambda a, b: lh_ref[:, a:b].astype(BF16)
    if has_vfirst:
        v = v + (vf_ref[...] - v) * _sigmoid(pv[4:5] + _dot(seg(_LH_V, _LH_END), v2_ref[...]))
    v_ref[...] = v
    bd = _head_bd()
    kk = k * pv[5:6]
    kk = kk * lax.rsqrt(jnp.maximum(_head_sums(kk * kk, bd), 1e-24))
    aux_ref[1] = _dot(seg(_LH_G, _LH_V), g2_ref[...]).astype(aux_ref.dtype)
    row = lax.broadcasted_iota(jnp.int32, (tm, tm), 0)
    col = lax.broadcasted_iota(jnp.int32, (tm, tm), 1)
    same = (row // R_CHUNK) == (col // R_CHUNK)
    hw = seg(_LH_W, _LH_A)
    ha = seg(_LH_A, _LH_G)
    ksum = jnp.zeros_like(k)
    nchunk = tm // R_CHUNK
    for d in range(2):
        lw = _sigmoid(pv[d:d + 1] + _dot(hw, w2_ref[d])) * (-_EXP_M05)
        a = _sigmoid(pv[2 + d:3 + d] + _dot(ha, a2_ref[d]))
        kd = k * (1.0 + (a - 1.0) * pv[6:7])
        bv = kk * a
        ksum = ksum + kd
        tri = jnp.where(same & ((col <= row) if d == 0 else (col >= row)), 1.0, 0.0).astype(BF16)
        cum = _dot_01_lhs(tri, lw)
        end = R_CHUNK - 1 if d == 0 else 0
        gt = jnp.exp(cum.reshape(nchunk, R_CHUNK, cum.shape[1])[:, end:end + 1, :])
        e_pos = jnp.exp(cum)
        e_neg = jnp.exp(-cum)
        e_end = (e_neg.reshape(nchunk, R_CHUNK, cum.shape[1]) * gt).reshape(cum.shape)
        t6_ref[d, 0] = (r * e_pos).astype(BF16)
        t6_ref[d, 1] = (-kk * jnp.exp(cum - lw)).astype(BF16)
        t6_ref[d, 2] = (kd * e_neg).astype(BF16)
        t6_ref[d, 3] = (bv * e_neg).astype(BF16)
        t6_ref[d, 4] = (kd * e_end).astype(BF16)
        t6_ref[d, 5] = (bv * e_end).astype(BF16)
        for c in range(nchunk):
            gt_ref[d, c] = gt[c]
    bonus = _head_sums(r * ksum * pv[7:8], bd) * v
    aux_ref[0] = bonus.astype(aux_ref.dtype)


def rw_gates(rkv, lh, v_first, w2, a2, g2, v2, pvec, *, tm=256, tc=1024):
    _, n, d = rkv.shape
    has_vf = v_first is not None
    tile = pl.BlockSpec((tm, tc), lambda i, j: (i, j))
    in_specs = [pl.BlockSpec((3, tm, tc), lambda i, j: (0, i, j)),
                pl.BlockSpec((tm, _LH_END), lambda i, j: (i, 0))]
    args = [rkv, lh]
    if has_vf:
        in_specs.append(tile)
        args.append(v_first)
    in_specs += [pl.BlockSpec((2, 256, tc), lambda i, j: (0, 0, j)),
                 pl.BlockSpec((2, 256, tc), lambda i, j: (0, 0, j)),
                 pl.BlockSpec((256, tc), lambda i, j: (0, j))]
    args += [w2, a2, g2]
    if has_vf:
        in_specs.append(pl.BlockSpec((128, tc), lambda i, j: (0, j)))
        args.append(v2)
    in_specs.append(pl.BlockSpec((8, tc), lambda i, j: (0, j)))
    args.append(pvec)
    nch = tm // R_CHUNK
    return pl.pallas_call(
        functools.partial(_rw_gate_kernel, tm=tm, has_vfirst=has_vf),
        out_shape=[jax.ShapeDtypeStruct((2, 6, n, d), BF16),
                   jax.ShapeDtypeStruct((2, n // R_CHUNK, 1, d), F32),
                   jax.ShapeDtypeStruct((n, d), F32),
                   jax.ShapeDtypeStruct((2, n, d), BF16)],
        grid=(n // tm, d // tc),
        in_specs=in_specs,
        out_specs=[pl.BlockSpec((2, 6, tm, tc), lambda i, j: (0, 0, i, j)),
                   pl.BlockSpec((2, nch, 1, tc), lambda i, j: (0, i, 0, j)),
                   tile,
                   pl.BlockSpec((2, tm, tc), lambda i, j: (0, i, j))],
        compiler_params=_cparams(("parallel", "parallel")),
        name="rw_gates",
    )(*args)


def _pair_stack(y, hi_lane):
    z = jnp.zeros_like(y)
    return jnp.concatenate([jnp.where(hi_lane, z, y), jnp.where(hi_lane, y, z)], axis=0)


def _rw_chunk_kernel(t6_ref, v_ref, gt_ref, ghq_ref, y0_ref, *, npair, unroll):
    d = pl.program_id(1)
    tc = R_CHUNK
    lane = lax.broadcasted_iota(jnp.int32, (tc, 128), 1)
    row = lax.broadcasted_iota(jnp.int32, (tc, 128), 0)
    hi_lane = lane >= R_HEAD
    rel = ((lane & (R_HEAD - 1)) - row) * (1 - 2 * d)
    strict = rel < 0
    incl = rel <= 0
    eye2 = jnp.where(rel == 0, 1.0, 0.0)

    stack = lambda y: _pair_stack(y.astype(BF16), hi_lane)
    rows = lambda a, b: jnp.concatenate([a.astype(BF16), b.astype(BF16)], axis=0)
    top, bot = (lambda m: m[:tc]), (lambda m: m[tc:])
    lft, rgt = (lambda m: m[:, :128]), (lambda m: m[:, 128:])

    def pm(a, y):
        return _dot(a.astype(BF16), stack(y))

    def pm2(a, y1, y2):
        return _dot(a.astype(BF16), jnp.concatenate([stack(y1), stack(y2)], axis=1))

    def pack_kv(full):
        return jnp.where(hi_lane, full[R_HEAD:], full[:R_HEAD])

    def body(it, carry):
        sls = [pl.ds(pl.multiple_of((it * unroll + u) * 128, 128), 128) for u in range(unroll)]
        each = lambda f, *cols: [f(*xs) for xs in zip(*cols)]
        rt, at, kt, bt, kh, bh = ([t6_ref[c, :, sl] for sl in sls] for c in range(6))
        v = [v_ref[:, sl].astype(BF16) for sl in sls]
        ar = each(rows, at, rt)
        xb = each(lambda x, y: _dot_nt(x, stack(y)), ar, bt)
        xk = each(lambda x, y: _dot_nt(x, stack(y)), ar, kt)
        lab = each(lambda m: jnp.where(strict, top(m), 0.0), xb)
        arb = each(lambda m: jnp.where(incl, bot(m), 0.0), xb)
        lak = each(lambda m: jnp.where(strict, top(m), 0.0), xk)
        ark = each(lambda m: jnp.where(incl, bot(m), 0.0), xk)
        inv = each(lambda l: eye2 + l, lab)
        pw = each(pm, lab, lab)
        for _ in range(4):
            both = each(lambda i, p: pm(rows(i, p), p), inv, pw)
            inv = each(lambda i, m: i + top(m), inv, both)
            pw = each(bot, both)
        inv = each(lambda i, p: i + pm(i, p), inv, pw)
        lv = each(lambda a, b, vv: pm(rows(a, b), vv), lak, ark, v)
        m2 = each(top, lv)
        wu = each(pm2, inv, at, m2)
        au = each(lambda a, m: pm2(a, lft(m), rgt(m)), arb, wu)
        q = each(lambda r, m: r.astype(F32) + lft(m), rt, au)
        y0 = each(lambda m, n: bot(m) + rgt(n), lv, au)
        bwu = each(lambda b, m: _dot_tn(b, m.astype(BF16)), bh, wu)
        g = each(lambda sl, m: jnp.where(rel == 0, gt_ref[:, sl], 0.0) + pack_kv(lft(m)), sls, bwu)
        h = each(lambda k, vv, m: pack_kv(_dot_tn(k, vv)) + pack_kv(rgt(m)), kh, v, bwu)
        for u, sl in enumerate(sls):
            ghq_ref[0, :, sl] = g[u].astype(BF16)
            ghq_ref[1, :, sl] = h[u].astype(BF16)
            ghq_ref[2, :, sl] = q[u].astype(BF16)
            y0_ref[:, sl] = y0[u]
        return carry

    lax.fori_loop(0, npair // unroll, body, 0)


def rw_chunk_ops(t6, v, gt):
    _, _, n, d = t6.shape
    tc = R_CHUNK
    return pl.pallas_call(
        functools.partial(_rw_chunk_kernel, npair=d // 128, unroll=16),
        out_shape=[jax.ShapeDtypeStruct((2, 3, n, d), BF16), jax.ShapeDtypeStruct((2, n, d), F32)],
        grid=(n // tc, 2),
        in_specs=[pl.BlockSpec((None, 6, tc, d), lambda c, e: (e, 0, c, 0)),
                  pl.BlockSpec((tc, d), lambda c, e: (c, 0)),
                  pl.BlockSpec((None, None, 1, d), lambda c, e: (e, c, 0, 0))],
        out_specs=[pl.BlockSpec((None, 3, tc, d), lambda c, e: (e, 0, c, 0)),
                   pl.BlockSpec((None, tc, d), lambda c, e: (e, c, 0))],
        compiler_params=_cparams(("parallel", "parallel")),
        name="rw_chunk_ops",
    )(t6, v, gt)


def _rw_scan_kernel(ghq_ref, y0_ref, y_ref, s_scr, *, npair):
    j = pl.program_id(2)

    @pl.when(j == 0)
    def _():
        s_scr[...] = jnp.zeros_like(s_scr)

    hi_lane = lax.broadcasted_iota(jnp.int32, (R_CHUNK, 128), 1) >= R_HEAD

    for p in range(npair):
        sl = slice(p * 128, (p + 1) * 128)
        s = s_scr[:, sl]
        s_hi = s.astype(BF16)
        s_lo = (s - s_hi.astype(F32)).astype(BF16)
        gq = jnp.concatenate([ghq_ref[0, :, sl], ghq_ref[2, :, sl]], axis=0)
        res = _dot(gq, _pair_stack(s_hi, hi_lane)) + _dot(gq, _pair_stack(s_lo, hi_lane))
        s_scr[:, sl] = res[:R_CHUNK] + ghq_ref[1, :, sl].astype(F32)
        y_ref[:, sl] = res[R_CHUNK:] + y0_ref[:, sl]


def rw_scan(ghq, y0, *, batch, lc, ctx):
    _, _, n, d = ghq.shape
    tc = R_CHUNK
    nc = lc // tc
    nc_ctx = ctx // tc
    rb = lambda b, e, j: b * nc + _chunk_order(e, j, nc, nc_ctx)
    return pl.pallas_call(
        functools.partial(_rw_scan_kernel, npair=d // 128),
        out_shape=jax.ShapeDtypeStruct((2, n, d), F32),
        grid=(batch, 2, nc),
        in_specs=[pl.BlockSpec((None, 3, tc, d), lambda b, e, j: (e, 0, rb(b, e, j), 0)),
                  pl.BlockSpec((None, tc, d), lambda b, e, j: (e, rb(b, e, j), 0))],
        out_specs=pl.BlockSpec((None, tc, d), lambda b, e, j: (e, rb(b, e, j), 0)),
        scratch_shapes=[pltpu.VMEM((tc, d), F32)],
        compiler_params=_cparams(("parallel", "parallel", "arbitrary")),
        name="rw_scan",
    )(ghq, y0)


def _rw_out_kernel(x_ref, g1_ref, y_ref, aux_ref, ln_ref, wo_ref, out_ref, z_scr, *, tm, tpb, ctx):
    i = pl.program_id(0)
    j = pl.program_id(1)

    @pl.when(j == 0)
    def _():
        bd = _head_bd()
        y = y_ref[0] + y_ref[1]
        mu = _head_sums(y, bd) * (1.0 / R_HEAD)
        yc = y - mu
        var = _head_sums(yc * yc, bd) * (1.0 / R_HEAD)
        z = yc * lax.rsqrt(var + R_LN_EPS) * ln_ref[0:1, :] + ln_ref[1:2, :]
        z_scr[...] = ((z + aux_ref[0].astype(F32)) * aux_ref[1].astype(F32)).astype(BF16)

    b = i // tpb
    is_ctx = ((i % tpb) * tm + _row_iota(tm)) < ctx
    out_ref[...] = x_ref[...] + _mod_rows(g1_ref, b, is_ctx) * _dot(z_scr[...], wo_ref[...])


def rw_out_block(x, mod, y, aux, ln_wb, w_o, *, lc, ctx, tm, tn=512):
    n, d = x.shape
    nj = d // tn
    return pl.pallas_call(
        functools.partial(_rw_out_kernel, tm=tm, tpb=lc // tm, ctx=ctx),
        out_shape=jax.ShapeDtypeStruct((n, d), F32),
        grid=(n // tm, nj),
        in_specs=[pl.BlockSpec((tm, tn), lambda i, j: (i, j)),
                  pl.BlockSpec((8, tn), lambda i, j: (0, 2 * nj + j)),
                  pl.BlockSpec((2, tm, d), lambda i, j: (0, i, 0)),
                  pl.BlockSpec((2, tm, d), lambda i, j: (0, i, 0)),
                  pl.BlockSpec((2, d), lambda i, j: (0, 0)),
                  pl.BlockSpec((d, tn), lambda i, j: (0, j))],
        out_specs=pl.BlockSpec((tm, tn), lambda i, j: (i, j)),
        scratch_shapes=[pltpu.VMEM((tm, d), BF16)],
        compiler_params=_cparams(("parallel", "arbitrary")),
        name="rw_out",
    )(x, mod, y, aux, ln_wb, w_o)


def odd_params(mu, w_r, w_k, w_v, w_o, w0, w1, w2, a0, a1, a2, g1, g2, k_k, k_a, r_k, ln_w, ln_b,
               v0=None, v1=None, v2=None):
    d = w_r.shape[0]
    dw, da, dg = w1.shape[-1], a1.shape[-1], g1.shape[-1]
    assert 2 * dw <= _LH_A - _LH_W and 2 * da <= _LH_G - _LH_A and dg <= _LH_V - _LH_G
    wl = jnp.zeros((d, _LH_END), F32)
    wl = wl.at[:, _LH_W:_LH_W + dw].set(w1[0]).at[:, _LH_W + dw:_LH_W + 2 * dw].set(w1[1])
    wl = wl.at[:, _LH_A:_LH_A + da].set(a1[0]).at[:, _LH_A + da:_LH_A + 2 * da].set(a1[1])
    wl = wl.at[:, _LH_G:_LH_G + dg].set(g1)
    w2p = jnp.zeros((2, 256, d), F32).at[0, :dw].set(w2[0]).at[1, dw:2 * dw].set(w2[1])
    a2p = jnp.zeros((2, 256, d), F32).at[0, :da].set(a2[0]).at[1, da:2 * da].set(a2[1])
    g2p = jnp.zeros((256, d), F32).at[:dg].set(g2)
    v2p = None
    vzero = jnp.zeros((d,), F32)
    if v1 is not None:
        dv = v1.shape[-1]
        assert dv <= _LH_END - _LH_V
        wl = wl.at[:, _LH_V:_LH_V + dv].set(v1)
        v2p = jnp.zeros((128, d), F32).at[:dv].set(v2).astype(BF16)
    pvec = jnp.stack([w0[0], w0[1], a0[0], a0[1], v0 if v0 is not None else vzero,
                      k_k, k_a, r_k.reshape(d)])
    return dict(mu=mu, w_lora1=wl.astype(BF16), w_rkv=jnp.stack([w_r, w_k, w_v]).astype(BF16),
                w2=w2p.astype(BF16), a2=a2p.astype(BF16), g2=g2p.astype(BF16), v2=v2p, pvec=pvec,
                ln_wb=jnp.stack([ln_w, ln_b]), w_o=w_o.astype(BF16))


def odd_layer(x, mod, nw, p, v_first, *, batch, lc, ctx, tm):
    rkv, lh = rw_project(x, mod, nw, p["mu"], p["w_lora1"], p["w_rkv"], lc=lc, ctx=ctx)
    t6, gt, v, aux = rw_gates(rkv, lh, v_first if p["v2"] is not None else None,
                              p["w2"], p["a2"], p["g2"], p["v2"], p["pvec"])
    ghq, y0 = rw_chunk_ops(t6, v, gt)
    y = rw_scan(ghq, y0, batch=batch, lc=lc, ctx=ctx)
    return rw_out_block(x, mod, y, aux, p["ln_wb"], p["w_o"], lc=lc, ctx=ctx, tm=tm), v


def _row_tile(lc, cap):
    return max(t for t in range(16, cap + 1, 16) if lc % t == 0)


def kernel(x, c, ctx, c_ctx, ada_w, ada_b, norm_mix, norm_ffn, ffn_w_up, ffn_conv_w, ffn_conv_b, ffn_w_down, norm_final, ev_w_in, ev_b_in, ev_w_out, s5_lam_re, s5_lam_im, s5_log_step, s5_b_re, s5_b_im, s5_c_re, s5_c_im, s5_d, s5_w_glu, s5_b_glu, ml_conv_w, ml_conv_b, ml_norm, rw_mu, rw_w_r, rw_w_k, rw_w_v, rw_w_o, rw_w0, rw_w1, rw_w2, rw_a0, rw_a1, rw_a2, rw_v0, rw_v1, rw_v2, rw_g1, rw_g2, rw_k_k, rw_k_a, rw_r_k, rw_ln_w, rw_ln_b):
    batch, seq, d = x.shape
    n_ctx = ctx.shape[1]
    lc = n_ctx + seq
    depth = ada_w.shape[0]
    n_even = ev_w_in.shape[0]
    assert batch < _CTX_ROW and n_ctx % 256 == 0 and seq % 256 == 0 and seq % GRID_W == 0
    tm = _row_tile(lc, 544)
    tm_small = _row_tile(lc, 272)

    xc = jnp.concatenate([ctx, x], axis=1).reshape(batch * lc, d)
    c8 = jnp.zeros((8, d), F32).at[:batch].set(c).at[_CTX_ROW].set(c_ctx)
    mods = adaln(c8, ada_w, ada_b)

    g, p_state = s5_lam_re.shape[-2:]
    flat = lambda a: a.reshape((n_even * 2,) + a.shape[2:])
    s5p = s5_params(flat(s5_lam_re), flat(s5_lam_im), s5_log_step.reshape(n_even * 2, g),
                    flat(s5_b_re), flat(s5_b_im), flat(s5_c_re), flat(s5_c_im))

    v_first = None
    for l in range(depth):
        j = l // 2
        if l % 2 == 0:
            ep = even_params(ev_w_in[j], ev_b_in[j], ev_w_out[j], s5_w_glu[j], s5_b_glu[j],
                             ml_conv_w[j], ml_conv_b[j], ml_norm[j])
            ops = s5_assemble(tuple(a[2 * j:2 * j + 2] for a in s5p), s5_d[j])
            xc = even_layer(xc, mods[l], norm_mix[l], ep, ops, batch=batch, lc=lc, ctx=n_ctx, tm=tm)
        else:
            extra = {} if j == 0 else dict(v0=rw_v0[j - 1], v1=rw_v1[j - 1], v2=rw_v2[j - 1])
            op = odd_params(rw_mu[j], rw_w_r[j], rw_w_k[j], rw_w_v[j], rw_w_o[j], rw_w0[j], rw_w1[j],
                            rw_w2[j], rw_a0[j], rw_a1[j], rw_a2[j], rw_g1[j], rw_g2[j], rw_k_k[j],
                            rw_k_a[j], rw_r_k[j], rw_ln_w[j], rw_ln_b[j], **extra)
            xc, v = odd_layer(xc, mods[l], norm_mix[l], op, v_first, batch=batch, lc=lc, ctx=n_ctx,
                              tm=tm_small)
            if j == 0:
                v_first = v
        xc = conv_ffn_block(xc, mods[l], norm_ffn[l], ffn_w_up[l].astype(BF16), ffn_conv_w[l],
                            ffn_conv_b[l], ffn_w_down[l].astype(BF16), lc=lc, ctx=n_ctx, tm=tm)
    out = final_norm(xc, norm_final, batch=batch, lc=lc, ctx=n_ctx)
    return out.reshape(batch, seq, d)
```

```python
import functools

import jax
import jax.numpy as jnp
from jax import lax
from jax.experimental import pallas as pl
from jax.experimental.pallas import tpu as pltpu

F32 = jnp.float32
BF16 = jnp.bfloat16
EPS = 1e-6
NEG_INF = -1e30
GRID_W = 64
S5_T = 16
M_HEADS = 4
M_CHUNK = 128
R_HEAD = 64
R_CHUNK = 64
R_LN_EPS = 64e-5
HIGHEST = lax.Precision.HIGHEST
VMEM_LIMIT = 56 * 1024 * 1024
_MXU_N = 256


def _cparams(sem):
    return pltpu.CompilerParams(dimension_semantics=sem, vmem_limit_bytes=VMEM_LIMIT)


def _dot(a, b):
    return jnp.dot(a, b, preferred_element_type=F32)


def _dot_nt(a, b):
    return lax.dot_general(a, b, (((1,), (1,)), ((), ())), preferred_element_type=F32)


def _dot_tn(a, b):
    return lax.dot_general(a, b, (((0,), (0,)), ((), ())), preferred_element_type=F32)


def _split3(x):
    h = x.astype(BF16)
    r = x - h.astype(F32)
    m = r.astype(BF16)
    l = (r - m.astype(F32)).astype(BF16)
    return h, m, l


def _dot_exact_lhs(a_bf16, x):
    h, m, l = _split3(x)
    return _dot(a_bf16, h) + _dot(a_bf16, m) + _dot(a_bf16, l)


def _dot_01_lhs(a_bf16, x):
    h = x.astype(BF16)
    l = (x - h.astype(F32)).astype(BF16)
    return _dot(a_bf16, h) + _dot(a_bf16, l)


_EXP_M05 = 0.6065306597126334


def _sigmoid(x):
    return jax.nn.sigmoid(x)


def _silu(x):
    return x * jax.nn.sigmoid(x)


def _gelu_tanh(x):
    return 0.5 * x * (1.0 + jnp.tanh(0.7978845608028654 * (x + 0.044715 * (x * x * x))))


def _row_iota(n):
    return lax.broadcasted_iota(jnp.int32, (n, 1), 0)


def _mod_rows(ref, b, is_ctx):
    lat = ref[pl.ds(b, 1), :]
    ctx = ref[pl.ds(_CTX_ROW, 1), :]
    return jnp.where(is_ctx, ctx, lat)


_CTX_ROW = 7


def _norm_mod(x, nw, sc, sh):
    ms = jnp.mean(x * x, axis=-1, keepdims=True)
    return x * lax.rsqrt(ms + EPS) * nw * (1.0 + sc) + sh


def _adaln_kernel(c_ref, w_ref, b_ref, o_ref):
    a = _silu(c_ref[...])
    o_ref[0] = _dot(a.astype(BF16), w_ref[0].astype(BF16)) + b_ref[0]


def adaln(c8, ada_w, ada_b, tn=1024):
    depth, d, n6 = ada_w.shape
    return pl.pallas_call(
        _adaln_kernel,
        out_shape=jax.ShapeDtypeStruct((depth, 8, n6), F32),
        grid=(depth, n6 // tn),
        in_specs=[pl.BlockSpec((8, d), lambda l, j: (0, 0)),
                  pl.BlockSpec((1, d, tn), lambda l, j: (l, 0, j)),
                  pl.BlockSpec((1, 1, tn), lambda l, j: (l, 0, j))],
        out_specs=pl.BlockSpec((1, 8, tn), lambda l, j: (l, 0, j)),
        compiler_params=_cparams(("parallel", "parallel")),
        name="adaln",
    )(c8, ada_w, ada_b.reshape(depth, 1, n6))


def _halo_specs(tm, d, n_rows, halo):
    r = tm // halo
    nblk = n_rows // halo
    return [pl.BlockSpec((tm, d), lambda i, j: (i, 0)),
            pl.BlockSpec((halo, d), lambda i, j: (jnp.maximum(i * r - 1, 0), 0)),
            pl.BlockSpec((halo, d), lambda i, j: (jnp.minimum(i * r + r, nblk - 1), 0))]


def _mod_spec(d, k):
    return pl.BlockSpec((8, d), lambda i, j: (0, k))


def _conv3(u, first, last, cw, cb):
    n = u.shape[0]
    up = jnp.where(first, 0.0, pltpu.roll(u, 1, 0))
    un = jnp.where(last, 0.0, pltpu.roll(u, n - 1, 0))
    return up * cw[0:1] + u * cw[1:2] + un * cw[2:3] + cb


def _ext_rows(xm_ref, xp_ref, xn_ref, sh_ref, sc_ref, nw_ref, i, tm, tpb, ctx, halo):
    b = i // tpb
    pos = (i % tpb) * tm - halo + _row_iota(tm + 2 * halo)
    is_ctx = pos < ctx
    xe = jnp.concatenate([xp_ref[...], xm_ref[...], xn_ref[...]], axis=0)
    h = _norm_mod(xe, nw_ref[...], _mod_rows(sc_ref, b, is_ctx), _mod_rows(sh_ref, b, is_ctx))
    return h, pos


def _ffn_kernel(xm_ref, xp_ref, xn_ref, sh_ref, sc_ref, g_ref, nw_ref, wa_ref, wg_ref,
                cwa_ref, cwg_ref, cba_ref, cbg_ref, wd_ref, o_ref, h_scr, act_scr, acc_scr,
                *, tm, nj, tpb, ctx, lc):
    i = pl.program_id(0)
    j = pl.program_id(1)

    hm = tm // 2
    wins = (0, hm)

    def up_dots():
        out = []
        for r0 in wins:
            h = h_scr[r0:r0 + hm + 16, :]
            out.append((_dot(h, wa_ref[...]), _dot(h, wg_ref[...])))
        return out

    def down_prev():
        acc_scr[...] += _dot(act_scr[(j + 1) % 2], wd_ref[...])

    def activate(us):
        for r0, (ua, ug) in zip(wins, us):
            pos = (i % tpb) * tm + r0 - 8 + _row_iota(hm + 16)
            first = (pos == 0) | (pos == ctx)
            last = (pos == ctx - 1) | (pos == lc - 1)
            a = _conv3(ua, first, last, cwa_ref[...], cba_ref[...])[8:8 + hm]
            g = _conv3(ug, first, last, cwg_ref[...], cbg_ref[...])[8:8 + hm]
            act_scr[j % 2, r0:r0 + hm, :] = (a * _silu(g)).astype(BF16)

    @pl.when(j == 0)
    def _():
        h, _ = _ext_rows(xm_ref, xp_ref, xn_ref, sh_ref, sc_ref, nw_ref, i, tm, tpb, ctx, 8)
        h_scr[...] = h.astype(BF16)
        acc_scr[...] = jnp.zeros_like(acc_scr)
        activate(up_dots())

    @pl.when((j > 0) & (j < nj))
    def _():
        us = up_dots()
        down_prev()
        activate(us)

    @pl.when(j == nj)
    def _():
        down_prev()
        b = i // tpb
        is_ctx = ((i % tpb) * tm + _row_iota(tm)) < ctx
        o_ref[...] = xm_ref[...] + _mod_rows(g_ref, b, is_ctx) * acc_scr[...]


def conv_ffn_block(x, mod, nw, w_up, conv_w, conv_b, w_down, *, lc, ctx, tm, tn=512):
    n, d = x.shape
    dff = w_down.shape[0]
    nj = dff // tn
    up = lambda j: jnp.minimum(j, nj - 1)
    dn = lambda j: jnp.maximum(j - 1, 0)
    kern = functools.partial(_ffn_kernel, tm=tm, nj=nj, tpb=lc // tm, ctx=ctx, lc=lc)
    return pl.pallas_call(
        kern,
        out_shape=jax.ShapeDtypeStruct((n, d), F32),
        grid=(n // tm, nj + 1),
        in_specs=_halo_specs(tm, d, n, 8) + [
            _mod_spec(d, 3), _mod_spec(d, 4), _mod_spec(d, 5),
            pl.BlockSpec((1, d), lambda i, j: (0, 0)),
            pl.BlockSpec((d, tn), lambda i, j: (0, up(j))),
            pl.BlockSpec((d, tn), lambda i, j: (0, up(j) + nj)),
            pl.BlockSpec((3, tn), lambda i, j: (0, up(j))),
            pl.BlockSpec((3, tn), lambda i, j: (0, up(j) + nj)),
            pl.BlockSpec((1, tn), lambda i, j: (0, up(j))),
            pl.BlockSpec((1, tn), lambda i, j: (0, up(j) + nj)),
            pl.BlockSpec((tn, d), lambda i, j: (dn(j), 0)),
        ],
        out_specs=pl.BlockSpec((tm, d), lambda i, j: (i, 0)),
        scratch_shapes=[pltpu.VMEM((tm + 16, d), BF16), pltpu.VMEM((2, tm, tn), BF16),
                        pltpu.VMEM((tm, d), F32)],
        compiler_params=_cparams(("parallel", "arbitrary")),
        name="conv_ffn",
    )(x, x, x, mod, mod, mod, nw.reshape(1, d), w_up, w_up, conv_w, conv_w,
      conv_b.reshape(1, -1), conv_b.reshape(1, -1), w_down)


def _final_norm_kernel(x_ref, w_ref, o_ref):
    x = x_ref[...]
    o_ref[...] = x * lax.rsqrt(jnp.mean(x * x, axis=-1, keepdims=True) + EPS) * w_ref[...]


def final_norm(x, w, *, batch, lc, ctx, tm=256):
    n, d = x.shape
    tpb = lc // tm
    skip = ctx // tm
    per = tpb - skip
    return pl.pallas_call(
        _final_norm_kernel,
        out_shape=jax.ShapeDtypeStruct((batch * per * tm, d), F32),
        grid=(batch, per),
        in_specs=[pl.BlockSpec((tm, d), lambda b, t: (b * tpb + skip + t, 0)),
                  pl.BlockSpec((1, d), lambda b, t: (0, 0))],
        out_specs=pl.BlockSpec((tm, d), lambda b, t: (b * per + t, 0)),
        compiler_params=_cparams(("parallel", "parallel")),
        name="final_norm",
    )(x, w.reshape(1, d))


def _even_in_kernel(xm_ref, xp_ref, xn_ref, sh_ref, sc_ref, nw_ref, wq_ref, bq_ref, cw_ref, cb_ref,
                    s_ref, wu_ref, bu_ref, wg_ref, bg_ref, qk_ref, uvo_ref, g_ref, h_scr,
                    *, tm, tpb, ctx, lc, nq, nu):
    i = pl.program_id(0)
    j = pl.program_id(1)

    @pl.when(j == 0)
    def _():
        h, _ = _ext_rows(xm_ref, xp_ref, xn_ref, sh_ref, sc_ref, nw_ref, i, tm, tpb, ctx, 8)
        h_scr[...] = h.astype(BF16)

    @pl.when(j < nq)
    def _():
        pos = (i % tpb) * tm - 8 + _row_iota(tm + 16)
        first = (pos == 0) | (pos == ctx)
        last = (pos == ctx - 1) | (pos == lc - 1)
        u = _dot(h_scr[...], wq_ref[...]) + bq_ref[...]
        u = _silu(_conv3(u, first, last, cw_ref[...], cb_ref[...])[8:8 + tm]) * s_ref[...]
        qk_ref[...] = u.astype(qk_ref.dtype)

    @pl.when((j >= nq) & (j < nq + nu))
    def _():
        uvo_ref[...] = (_dot(h_scr[8:8 + tm, :], wu_ref[...]) + bu_ref[...]).astype(uvo_ref.dtype)

    @pl.when(j == nq + nu)
    def _():
        g_ref[...] = _dot(h_scr[8:8 + tm, :], wg_ref[...]) + bg_ref[...]


def even_in_proj(x, mod, nw, p, *, lc, ctx, tm, tn=512):
    n, d = x.shape
    nq = p["w_qk"].shape[1] // tn
    nu = p["w_uvo"].shape[1] // tn
    cq = lambda j: jnp.minimum(j, nq - 1)
    cu = lambda j: jnp.clip(j - nq, 0, nu - 1)
    colq = lambda r: pl.BlockSpec((r, tn), lambda i, j: (0, cq(j)))
    row = lambda a: a.reshape(1, -1)
    return pl.pallas_call(
        functools.partial(_even_in_kernel, tm=tm, tpb=lc // tm, ctx=ctx, lc=lc, nq=nq, nu=nu),
        out_shape=[jax.ShapeDtypeStruct((n, nq * tn), BF16), jax.ShapeDtypeStruct((n, nu * tn), BF16),
                   jax.ShapeDtypeStruct((n, 128), F32)],
        grid=(n // tm, nq + nu + 1),
        in_specs=_halo_specs(tm, d, n, 8) + [
            _mod_spec(d, 0), _mod_spec(d, 1), pl.BlockSpec((1, d), lambda i, j: (0, 0)),
            pl.BlockSpec((d, tn), lambda i, j: (0, cq(j))), colq(1), colq(3), colq(1), colq(1),
            pl.BlockSpec((d, tn), lambda i, j: (0, cu(j))),
            pl.BlockSpec((1, tn), lambda i, j: (0, cu(j))),
            pl.BlockSpec((d, 128), lambda i, j: (0, 0)),
            pl.BlockSpec((1, 128), lambda i, j: (0, 0))],
        out_specs=[pl.BlockSpec((tm, tn), lambda i, j: (i, cq(j))),
                   pl.BlockSpec((tm, tn), lambda i, j: (i, cu(j))),
                   pl.BlockSpec((tm, 128), lambda i, j: (i, 0))],
        scratch_shapes=[pltpu.VMEM((tm + 16, d), BF16)],
        compiler_params=_cparams(("parallel", "arbitrary")),
        name="even_in_proj",
    )(x, x, x, mod, mod, nw.reshape(1, d), p["w_qk"], row(p["b_qk"]), p["conv_w"], row(p["conv_b"]),
      row(p["qk_scale"]), p["w_uvo"], row(p["b_uvo"]), p["w_gate"], row(p["b_gate"]))


def _s5_param_kernel(lr_ref, li_ref, ls_ref, bbr_ref, bbi_ref, cr_ref, ci_ref,
                     wre_ref, wim_ref, ere_ref, eim_ref, k_ref, lam_re_ref, lam_im_ref, *, t, ch):
    step = jnp.exp(ls_ref[0])
    lr = lr_ref[0]
    li = li_ref[0]
    p = lr.shape[-1]
    sr = lr * step
    si = li * step

    def power(tau):
        mag = jnp.exp(tau * sr)
        return mag * jnp.cos(tau * si), mag * jnp.sin(tau * si)

    ab_re, ab_im = power(1.0)
    den = lr * lr + li * li
    co_re = ((ab_re - 1.0) * lr + ab_im * li) / den
    co_im = (ab_im * lr - (ab_re - 1.0) * li) / den
    b_re = bbr_ref[0]
    b_im = bbi_ref[0]
    bb_re = co_re * b_re - co_im * b_im
    bb_im = co_re * b_im + co_im * b_re
    tile = lambda m: jnp.concatenate([m] * t, axis=0)
    tau = (lax.broadcasted_iota(jnp.int32, (t * ch, p), 0) // ch).astype(F32)
    pr, pi = power(tau)
    bbr_t, bbi_t = tile(bb_re), tile(bb_im)
    w_re = pr * bbr_t - pi * bbi_t
    w_im = pr * bbi_t + pi * bbr_t
    wre_ref[0] = w_re
    wim_ref[0] = w_im
    qr, qi = pr * ab_re - pi * ab_im, pr * ab_im + pi * ab_re
    c_re = cr_ref[0]
    c_im = ci_ref[0]
    cr_t, ci_t = tile(c_re), tile(c_im)
    ere_ref[0] = cr_t * qr - ci_t * qi
    eim_ref[0] = -(cr_t * qi + ci_t * qr)
    k_ref[0] = (lax.dot_general(c_re, w_re, (((1,), (1,)), ((), ())), precision=HIGHEST,
                                preferred_element_type=F32)
                - lax.dot_general(c_im, w_im, (((1,), (1,)), ((), ())), precision=HIGHEST,
                                  preferred_element_type=F32))
    lt_re, lt_im = power(float(t))
    lam_re_ref[0] = lt_re
    lam_im_ref[0] = lt_im


def s5_params(lam_re, lam_im, log_step, b_re, b_im, c_re, c_im, t=S5_T):
    nd, g, p = lam_re.shape
    ch = c_re.shape[2]
    m = nd * g
    r3 = lambda a: a.reshape(m, 1, p)
    bt = lambda a: jnp.swapaxes(a, -1, -2).reshape(m, ch, p)
    vec = pl.BlockSpec((1, 1, p), lambda i: (i, 0, 0))
    mat = pl.BlockSpec((1, ch, p), lambda i: (i, 0, 0))
    big = pl.BlockSpec((1, t * ch, p), lambda i: (i, 0, 0))
    outs = pl.pallas_call(
        functools.partial(_s5_param_kernel, t=t, ch=ch),
        out_shape=[jax.ShapeDtypeStruct((m, t * ch, p), F32)] * 4
        + [jax.ShapeDtypeStruct((m, ch, t * ch), F32)] + [jax.ShapeDtypeStruct((m, 1, p), F32)] * 2,
        grid=(m,),
        in_specs=[vec, vec, pl.BlockSpec((1, 1, 1), lambda i: (i, 0, 0)), mat, mat, mat, mat],
        out_specs=[big] * 4 + [pl.BlockSpec((1, ch, t * ch), lambda i: (i, 0, 0)), vec, vec],
        compiler_params=_cparams(("parallel",)),
        name="s5_params",
    )(r3(lam_re), r3(lam_im), log_step.reshape(m, 1, 1), bt(b_re), bt(b_im),
      c_re.reshape(m, ch, p), c_im.reshape(m, ch, p))
    w_re, w_im, e_re, e_im, k, l_re, l_im = outs
    sh = lambda a: a.reshape(nd, g, t, ch, p)
    return (sh(w_re), sh(w_im), sh(e_re), sh(e_im), k.reshape(nd, g, ch, t, ch),
            l_re.reshape(nd, g, p), l_im.reshape(nd, g, p))


def s5_assemble(params, d_skip, t=S5_T):
    w_re, w_im, e_re, e_im, k, l_re, l_im = params
    nd, g, _, ch, p = w_re.shape
    tc = t * ch
    s_i = jnp.arange(t)[:, None]
    t_i = jnp.arange(t)[None, :]
    kf = jnp.take(k[0], jnp.clip(t_i - s_i, 0, t - 1), axis=2)
    kb = jnp.take(k[1], jnp.clip(s_i - t_i, 0, t - 1), axis=2)
    mf = jnp.where((t_i >= s_i)[None, None, :, :, None], kf, 0.0)
    mb = jnp.where((s_i >= t_i)[None, None, :, :, None], kb, 0.0)
    m = jnp.transpose(mf + mb, (0, 2, 4, 3, 1))
    eye = (jnp.eye(t)[:, None, :, None] * jnp.eye(ch)[None, :, None, :])
    m = m + eye[None] * d_skip.reshape(g, 1, 1, 1, ch)
    m = m.reshape(g, tc, tc)
    def fmat(wr, wi):
        wr = wr.reshape(g, tc, p)
        wi = wi.reshape(g, tc, p)
        return jnp.concatenate([wr, wi, wi, wr], axis=-1)
    f2 = jnp.concatenate([fmat(w_re[0][:, ::-1], w_im[0][:, ::-1]), fmat(w_re[1], w_im[1])], axis=-1)
    def emat(er, ei):
        return jnp.concatenate([jnp.swapaxes(er.reshape(g, tc, p), 1, 2),
                                jnp.swapaxes(ei.reshape(g, tc, p), 1, 2)], axis=1)
    e2 = jnp.concatenate([emat(e_re[0], e_im[0]), emat(e_re[1][:, ::-1], e_im[1][:, ::-1])], axis=1)
    def coef(lr, li):
        a = jnp.concatenate([lr, lr], axis=-1)
        b1 = jnp.concatenate([-li, li], axis=-1)
        b2 = jnp.concatenate([li, -li], axis=-1)
        return jnp.concatenate([a, a], axis=-1), jnp.concatenate([b1, b2], axis=-1)
    af, bf = coef(l_re[0], l_im[0])
    ab, bb = coef(l_re[1], l_im[1])
    coefs = jnp.stack([jnp.concatenate([af, ab], axis=-1), jnp.concatenate([bf, bb], axis=-1)], axis=1)
    return m.astype(BF16), f2.astype(BF16), e2.astype(BF16), coefs


def _s5_in_kernel(u_ref, f_ref, z_ref):
    z_ref[...] = _dot(u_ref[0], f_ref[0])


def s5_chunk_inputs(ut, f2):
    g, r, tc = ut.shape
    w = f2.shape[-1]
    return pl.pallas_call(
        _s5_in_kernel,
        out_shape=jax.ShapeDtypeStruct((r, g * w), F32),
        grid=(g,),
        in_specs=[pl.BlockSpec((1, r, tc), lambda i: (i, 0, 0)),
                  pl.BlockSpec((1, tc, w), lambda i: (i, 0, 0))],
        out_specs=pl.BlockSpec((r, w), lambda i: (0, i)),
        compiler_params=_cparams(("parallel",)),
        name="s5_chunk_inputs",
    )(ut, f2)


def _s5_scan_kernel(z_ref, c_ref, h_ref, *, gb, nc, nc_ctx, lw):
    nb = z_ref.shape[0]
    ng, ng_ctx = nc // 8, nc_ctx // 8

    def body(jg, carry):
        rows = (pl.multiple_of(jg * 8, 8),
                pl.multiple_of(_chunk_order(1, jg, ng, ng_ctx) * 8, 8))
        new = list(carry)
        for gi in range(gb):
            for di in range(2):
                idx = 2 * (2 * gi + di)
                hs, hx = new[idx], new[idx + 1]
                base = (gi * 4 + 2 * di) * lw
                z8 = z_ref[:, pl.ds(rows[di], 8), base:base + lw]
                zx8 = z_ref[:, pl.ds(rows[di], 8), base + lw:base + 2 * lw]
                a = c_ref[gi, 0:1, 2 * di * lw:(2 * di + 1) * lw]
                b1 = c_ref[gi, 1:2, 2 * di * lw:(2 * di + 1) * lw]
                b2 = c_ref[gi, 1:2, (2 * di + 1) * lw:(2 * di + 2) * lw]
                entry = [None] * 8
                for s in range(8):
                    r = s if di == 0 else 7 - s
                    entry[r] = hs
                    hs, hx = (a * hs + b1 * hx + z8[:, r:r + 1, :],
                              a * hx + b2 * hs + zx8[:, r:r + 1, :])
                new[idx], new[idx + 1] = hs, hx
                h_ref[:, pl.ds(rows[di], 8), (gi * 2 + di) * lw:(gi * 2 + di + 1) * lw] = (
                    jnp.concatenate(entry, axis=1))
        return tuple(new)

    init = tuple(jnp.zeros((nb, 1, lw), F32) for _ in range(4 * gb))
    lax.fori_loop(0, ng, body, init)


def s5_chunk_scan(z, coefs, *, batch, nc, nc_ctx, gb=2):
    r, wtot = z.shape
    g = coefs.shape[0]
    lw = wtot // g // 4
    assert nc % 8 == 0 and nc_ctx % 8 == 0
    z3 = z.reshape(batch, nc, wtot)
    return pl.pallas_call(
        functools.partial(_s5_scan_kernel, gb=gb, nc=nc, nc_ctx=nc_ctx, lw=lw),
        out_shape=jax.ShapeDtypeStruct((batch, nc, g * 2 * lw), F32),
        grid=(g // gb,),
        in_specs=[pl.BlockSpec((batch, nc, gb * 4 * lw), lambda i: (0, 0, i)),
                  pl.BlockSpec((gb, 2, 4 * lw), lambda i: (i, 0, 0))],
        out_specs=pl.BlockSpec((batch, nc, gb * 2 * lw), lambda i: (0, 0, i)),
        compiler_params=_cparams(("parallel",)),
        name="s5_chunk_scan",
    )(z3, coefs).reshape(r, g * 2 * lw)


def _s5_out_kernel(u_ref, m_ref, h_ref, e_ref, y_ref):
    y = _dot(u_ref[0], m_ref[0]) + _dot(h_ref[...].astype(BF16), e_ref[0])
    y_ref[0] = y.astype(y_ref.dtype)


def s5_chunk_outputs(ut, m, hs, e2):
    g, r, tc = ut.shape
    hw = e2.shape[1]
    return pl.pallas_call(
        _s5_out_kernel,
        out_shape=jax.ShapeDtypeStruct((g, r, tc), BF16),
        grid=(g,),
        in_specs=[pl.BlockSpec((1, r, tc), lambda i: (i, 0, 0)),
                  pl.BlockSpec((1, tc, tc), lambda i: (i, 0, 0)),
                  pl.BlockSpec((r, hw), lambda i: (0, i)),
                  pl.BlockSpec((1, hw, tc), lambda i: (i, 0, 0))],
        out_specs=pl.BlockSpec((1, r, tc), lambda i: (i, 0, 0)),
        compiler_params=_cparams(("parallel",)),
        name="s5_chunk_outputs",
    )(ut, m, hs, e2)


def s5_mix(u, ops, *, batch, lc, ctx, t=S5_T):
    m, f2, e2, coefs = ops
    n, w = u.shape
    g = m.shape[0]
    ch = w // g
    r = n // t
    ut = jnp.transpose(u.reshape(r, t, g, ch), (2, 0, 1, 3)).reshape(g, r, t * ch)
    z = s5_chunk_inputs(ut, f2)
    hs = s5_chunk_scan(z, coefs, batch=batch, nc=lc // t, nc_ctx=ctx // t)
    yt = s5_chunk_outputs(ut, m, hs, e2)
    return jnp.transpose(yt.reshape(g, r, t, ch), (1, 2, 0, 3)).reshape(n, w)


def _chunk_order(d, j, nc, nc_ctx):
    bwd = jnp.where(j < nc_ctx, nc_ctx - 1 - j, nc - 1 - (j - nc_ctx))
    return jnp.where(d == 0, j, bwd)


def _mlstm_kernel(qkf_ref, vf_ref, gf_ref, qkb_ref, vb_ref, gb_ref, of_ref, ob_ref,
                  c_scr, n_scr, m_scr, *, nh):
    j = pl.program_id(1)
    tc = qkf_ref.shape[0]
    mw = vf_ref.shape[1]
    dh = mw // nh

    @pl.when(j == 0)
    def _():
        c_scr[...] = jnp.zeros_like(c_scr)
        n_scr[...] = jnp.zeros_like(n_scr)
        m_scr[...] = jnp.full_like(m_scr, NEG_INF)

    row = lax.broadcasted_iota(jnp.int32, (tc, tc), 0)
    col = lax.broadcasted_iota(jnp.int32, (tc, tc), 1)
    lane = lax.broadcasted_iota(jnp.int32, (tc, 128), 1)
    chains = [(d, hd) for d in range(2) for hd in range(nh)]
    each = lambda f, *cols: [f(*xs) for xs in zip(*cols)]
    qk_refs, v_refs, g_vals = (qkf_ref, qkb_ref), (vf_ref, vb_ref), (gf_ref[...], gb_ref[...])
    vis_d = [col <= row, col >= row]
    tri_d = [jnp.where(m, 1.0, 0.0).astype(BF16) for m in vis_d]
    pick = lambda d, c: jnp.sum(jnp.where(lane == c, g_vals[d], 0.0), axis=1, keepdims=True)
    ic = [pick(d, d * 2 * nh + hd) for d, hd in chains]
    fc = [pick(d, d * 2 * nh + nh + hd) for d, hd in chains]
    lf = each(lambda f: jnp.minimum(f, 0.0) - jnp.log(1.0 + jnp.exp(-jnp.abs(f))), fc)
    b1 = []
    for d in range(2):
        cat = jnp.concatenate([jnp.broadcast_to(lf[d * nh + hd], (tc, tc)) for hd in range(nh)], axis=1)
        cum = _dot_exact_lhs(tri_d[d], cat)
        b1 += [cum[:, hd * tc:(hd + 1) * tc] for hd in range(nh)]
    b2 = each(lambda m: m.T, b1)
    ic2 = each(lambda c: jnp.broadcast_to(c, (tc, tc)).T, ic)
    total = each(lambda f: jnp.sum(f, axis=0, keepdims=True), lf)
    m_prev = [m_scr[c, 0:1, 0:1] for c in range(len(chains))]
    logw = [jnp.where(vis_d[d], b1[c] - b2[c] + ic2[c], NEG_INF) for c, (d, _) in enumerate(chains)]
    bcol = each(lambda m: m[:, 0:1], b1)
    inter = each(lambda b, m: b + m, bcol, m_prev)
    m_row = each(lambda lw, it: jnp.maximum(jnp.max(lw, axis=1, keepdims=True), it), logw, inter)
    q = [qk_refs[d][:, hd * dh:(hd + 1) * dh] for d, hd in chains]
    k = [qk_refs[d][:, mw + hd * dh:mw + (hd + 1) * dh] for d, hd in chains]
    v = [v_refs[d][:, hd * dh:(hd + 1) * dh] for d, hd in chains]
    s = each(lambda qq, kk, lw, mr: _dot_nt(qq, kk) * jnp.exp(lw - mr), q, k, logw, m_row)
    w_inter = each(lambda it, mr: jnp.exp(it - mr), inter, m_row)
    c_old = [c_scr[c] for c in range(len(chains))]
    n_old = [n_scr[c, 0:1, :] for c in range(len(chains))]
    num = each(lambda ss, vv, wi, qq, co: _dot(ss.astype(BF16), vv) + wi * _dot(qq, co.astype(BF16)),
               s, v, w_inter, q, c_old)
    den = each(lambda ss, wi, qq, no: jnp.sum(ss, axis=1, keepdims=True)
               + wi * jnp.sum(qq.astype(F32) * no, axis=1, keepdims=True), s, w_inter, q, n_old)
    h = each(lambda nu, de, mr: nu / jnp.maximum(jnp.abs(de), jnp.exp(-mr)), num, den, m_row)
    of_ref[...] = jnp.concatenate(h[:nh], axis=1).astype(of_ref.dtype)
    ob_ref[...] = jnp.concatenate(h[nh:], axis=1).astype(ob_ref.dtype)
    lws = each(lambda t, b, c: t - b + c, total, bcol, ic)
    m_new = each(lambda t, m, l: jnp.maximum(t + m, jnp.max(l, axis=0, keepdims=True)), total, m_prev, lws)
    ek = each(lambda l, m, kk: jnp.exp(l - m) * kk.astype(F32), lws, m_new, k)
    cw = each(lambda t, m, mn: jnp.exp(t + m - mn), total, m_prev, m_new)
    c_new = each(lambda w, co, e, vv: w * co + _dot_tn(e.astype(BF16), vv), cw, c_old, ek, v)
    for c in range(len(chains)):
        c_scr[c] = c_new[c]
        n_scr[c, 0:1, :] = cw[c] * n_old[c] + jnp.sum(ek[c], axis=0, keepdims=True)
        m_scr[c] = jnp.broadcast_to(m_new[c], m_scr.shape[1:])


def mlstm_mix(qk, uvo, gates, *, batch, lc, ctx, nh=M_HEADS, tc=M_CHUNK):
    n = qk.shape[0]
    mw = qk.shape[1] // 2
    dh = mw // nh
    nc = lc // tc
    nc_ctx = ctx // tc
    assert uvo.shape[1] == 3 * mw
    rb = lambda d: (lambda b, j: b * nc + _chunk_order(d, j, nc, nc_ctx))
    ins, args = [], []
    for d in range(2):
        ins += [pl.BlockSpec((tc, 2 * mw), lambda b, j, r=rb(d): (r(b, j), 0)),
                pl.BlockSpec((tc, mw), lambda b, j, r=rb(d): (r(b, j), 1)),
                pl.BlockSpec((tc, 128), lambda b, j, r=rb(d): (r(b, j), 0))]
        args += [qk, uvo, gates]
    return pl.pallas_call(
        functools.partial(_mlstm_kernel, nh=nh),
        out_shape=[jax.ShapeDtypeStruct((n, mw), F32)] * 2,
        grid=(batch, nc),
        in_specs=ins,
        out_specs=[pl.BlockSpec((tc, mw), lambda b, j, r=rb(d): (r(b, j), 0)) for d in range(2)],
        scratch_shapes=[pltpu.VMEM((2 * nh, dh, dh), F32), pltpu.VMEM((2 * nh, 8, dh), F32),
                        pltpu.VMEM((2 * nh, 8, 128), F32)],
        compiler_params=_cparams(("parallel", "arbitrary")),
        name="mlstm",
    )(*args)


def _even_out_kernel(x_ref, g1_ref, y_ref, hf_ref, hb_ref, o_ref, mn_ref, wg_ref, bg_ref, wo_ref,
                     out_ref, mix_scr, *, tm, tpb, ctx, nh):
    i = pl.program_id(0)
    j = pl.program_id(1)

    @pl.when(j == 0)
    def _():
        s = _gelu_tanh(y_ref[...].astype(F32))
        glu = s * _sigmoid(_dot(s.astype(BF16), wg_ref[...]) + bg_ref[...])
        hm = hf_ref[...] + hb_ref[...]
        dh = hm.shape[1] // nh
        parts = []
        for hd in range(nh):
            seg = hm[:, hd * dh:(hd + 1) * dh]
            parts.append(seg * lax.rsqrt(jnp.mean(seg * seg, axis=-1, keepdims=True) + EPS))
        ml = jnp.concatenate(parts, axis=1) * mn_ref[...] * _sigmoid(o_ref[...].astype(F32))
        mix_scr[...] = jnp.concatenate([glu, ml], axis=1).astype(BF16)

    b = i // tpb
    is_ctx = ((i % tpb) * tm + _row_iota(tm)) < ctx
    out_ref[...] = x_ref[...] + _mod_rows(g1_ref, b, is_ctx) * _dot(mix_scr[...], wo_ref[...])


def even_out_block(x, mod, ys5, hf, hb, uvo, ml_norm, w_glu, b_glu, w_out, *, lc, ctx, tm, tn=512, nh=M_HEADS):
    n, d = x.shape
    sw = ys5.shape[1]
    mw = hf.shape[1]
    assert sw == mw
    nj = d // tn
    return pl.pallas_call(
        functools.partial(_even_out_kernel, tm=tm, tpb=lc // tm, ctx=ctx, nh=nh),
        out_shape=jax.ShapeDtypeStruct((n, d), F32),
        grid=(n // tm, nj),
        in_specs=[pl.BlockSpec((tm, tn), lambda i, j: (i, j)),
                  pl.BlockSpec((8, tn), lambda i, j: (0, 2 * nj + j)),
                  pl.BlockSpec((tm, sw), lambda i, j: (i, 0)),
                  pl.BlockSpec((tm, mw), lambda i, j: (i, 0)),
                  pl.BlockSpec((tm, mw), lambda i, j: (i, 0)),
                  pl.BlockSpec((tm, mw), lambda i, j: (i, 2)),
                  pl.BlockSpec((1, mw), lambda i, j: (0, 0)),
                  pl.BlockSpec((sw, sw), lambda i, j: (0, 0)),
                  pl.BlockSpec((1, sw), lambda i, j: (0, 0)),
                  pl.BlockSpec((sw + mw, tn), lambda i, j: (0, j))],
        out_specs=pl.BlockSpec((tm, tn), lambda i, j: (i, j)),
        scratch_shapes=[pltpu.VMEM((tm, sw + mw), BF16)],
        compiler_params=_cparams(("parallel", "arbitrary")),
        name="even_out",
    )(x, mod, ys5, hf, hb, uvo, ml_norm.reshape(1, mw), w_glu, b_glu.reshape(1, sw), w_out)


def even_layer(x, mod, nw, p, s5_ops, *, batch, lc, ctx, tm):
    qk, uvo, gates = even_in_proj(x, mod, nw, p, lc=lc, ctx=ctx, tm=tm)
    sw = p["w_glu"].shape[0]
    ys5 = s5_mix(uvo[:, :sw], s5_ops, batch=batch, lc=lc, ctx=ctx)
    hf, hb = mlstm_mix(qk, uvo, gates, batch=batch, lc=lc, ctx=ctx)
    return even_out_block(x, mod, ys5, hf, hb, uvo, p["ml_norm"], p["w_glu"], p["b_glu"], p["w_out"],
                          lc=lc, ctx=ctx, tm=tm)


def even_params(w_in, b_in, w_out, w_glu, b_glu, conv_w, conv_b, ml_norm, nh=M_HEADS):
    sw = w_glu.shape[0]
    mw = ml_norm.shape[0]
    c0, c1, c2 = sw, sw + 2 * mw, sw + 4 * mw
    ng = w_in.shape[1] - c2
    scale = jnp.concatenate([jnp.ones((mw,), F32), jnp.full((mw,), (mw // nh) ** -0.5, F32)])
    uvo_cols = lambda a: jnp.concatenate([a[..., :c0], a[..., c1:c2]], axis=-1)
    return dict(
        w_qk=w_in[:, c0:c1].astype(BF16), b_qk=b_in[c0:c1], qk_scale=scale,
        w_uvo=uvo_cols(w_in).astype(BF16), b_uvo=uvo_cols(b_in),
        w_gate=jnp.pad(w_in[:, c2:], ((0, 0), (0, 128 - ng))).astype(BF16),
        b_gate=jnp.pad(b_in[c2:], (0, 128 - ng)),
        conv_w=conv_w, conv_b=conv_b, ml_norm=ml_norm,
        w_glu=w_glu.astype(BF16), b_glu=b_glu, w_out=w_out.astype(BF16))


_LH_W, _LH_A, _LH_G, _LH_V, _LH_END = 0, 256, 512, 768, 896


def _rw_proj_kernel(xm_ref, xp_ref, xn_ref, sh_ref, sc_ref, nw_ref, mu_ref, wl_ref, w_ref,
                    rkv_ref, lh_ref, mix_scr, *, tm, tn, tpb, ctx, lc, nb):
    i = pl.program_id(0)
    j = pl.program_id(1)
    halo = GRID_W

    @pl.when(j == 0)
    def _():
        he, _ = _ext_rows(xm_ref, xp_ref, xn_ref, sh_ref, sc_ref, nw_ref, i, tm, tpb, ctx, halo)
        n = tm + 2 * halo
        d = he.shape[1]
        q = d // 4
        h = he[halo:halo + tm]
        hprev = pltpu.roll(he, 1, 0)[halo:halo + tm]
        hnext = pltpu.roll(he, n - 1, 0)[halo:halo + tm]
        hup = he[0:tm]
        hdown = he[2 * halo:2 * halo + tm]
        pos = (i % tpb) * tm + _row_iota(tm)
        is_ctx = pos < ctx
        pl_ = pos - ctx
        gcol = pl_ & (GRID_W - 1)
        ok_prev = jnp.where(is_ctx, pos, gcol) != 0
        ok_q1 = jnp.where(is_ctx, pos, gcol - (GRID_W - 1)) != 0
        ok_q2 = jnp.where(is_ctx, pos - (ctx - 1), jnp.maximum(pl_ - (GRID_W - 1), 0)) != 0
        ok_q3 = jnp.where(is_ctx, pos - (ctx - 1), jnp.maximum((lc - ctx) - GRID_W - pl_, 0)) != 0
        s0 = jnp.where(ok_prev, hprev[:, :q], 0.0)
        s1 = jnp.where(ok_q1, jnp.where(is_ctx, hprev[:, q:2 * q], hnext[:, q:2 * q]), 0.0)
        s2 = jnp.where(ok_q2, jnp.where(is_ctx, hnext[:, 2 * q:3 * q], hup[:, 2 * q:3 * q]), 0.0)
        s3 = jnp.where(ok_q3, jnp.where(is_ctx, hnext[:, 3 * q:], hdown[:, 3 * q:]), 0.0)
        xx = jnp.concatenate([s0, s1, s2, s3], axis=1) - h
        mix = lambda r: (h + xx * mu_ref[r:r + 1, :]).astype(BF16)
        xv = mix(3)
        mix_scr[0] = mix(0)
        mix_scr[1] = mix(2)
        mix_scr[2] = xv
        lh_ref[:, _LH_W:_LH_A] = jnp.tanh(_dot(mix(1), wl_ref[:, _LH_W:_LH_A]))
        lh_ref[:, _LH_A:_LH_G] = _dot(mix(4), wl_ref[:, _LH_A:_LH_G])
        lh_ref[:, _LH_G:_LH_V] = _sigmoid(_dot(mix(5), wl_ref[:, _LH_G:_LH_V]))
        lh_ref[:, _LH_V:_LH_END] = _dot(xv, wl_ref[:, _LH_V:_LH_END])

    cols = pl.ds(pl.multiple_of((j % nb) * tn, tn), tn)
    rkv_ref[...] = _dot(mix_scr[j // nb], w_ref[j // nb, :, cols])


def rw_project(x, mod, nw, mu, w_lora1, w_rkv, *, lc, ctx, tm=256, tn=512):
    n, d = x.shape
    nb = d // tn
    return pl.pallas_call(
        functools.partial(_rw_proj_kernel, tm=tm, tn=tn, tpb=lc // tm, ctx=ctx, lc=lc, nb=nb),
        out_shape=[jax.ShapeDtypeStruct((3, n, d), F32), jax.ShapeDtypeStruct((n, _LH_END), F32)],
        grid=(n // tm, 3 * nb),
        in_specs=_halo_specs(tm, d, n, GRID_W) + [
            _mod_spec(d, 0), _mod_spec(d, 1), pl.BlockSpec((1, d), lambda i, j: (0, 0)),
            pl.BlockSpec((6, d), lambda i, j: (0, 0)),
            pl.BlockSpec((d, _LH_END), lambda i, j: (0, 0)),
            pl.BlockSpec((3, d, d), lambda i, j: (0, 0, 0))],
        out_specs=[pl.BlockSpec((None, tm, tn), lambda i, j: (j // nb, i, j % nb)),
                   pl.BlockSpec((tm, _LH_END), lambda i, j: (i, 0))],
        scratch_shapes=[pltpu.VMEM((3, tm, d), BF16)],
        compiler_params=_cparams(("parallel", "arbitrary")),
        name="rw_project",
    )(x, x, x, mod, mod, nw.reshape(1, d), mu, w_lora1, w_rkv)


def _head_sums(x, bd):
    w = bd.shape[0]
    hi = x.astype(BF16)
    lo = (x - hi.astype(F32)).astype(BF16)
    parts = []
    for c in range(x.shape[1] // w):
        sl = slice(c * w, (c + 1) * w)
        both = _dot(jnp.concatenate([hi[:, sl], lo[:, sl]], axis=0), bd)
        parts.append(both[:x.shape[0]] + both[x.shape[0]:])
    return jnp.concatenate(parts, axis=1)


def _head_bd():
    r = lax.broadcasted_iota(jnp.int32, (_MXU_N, _MXU_N), 0) // R_HEAD
    c = lax.broadcasted_iota(jnp.int32, (_MXU_N, _MXU_N), 1) // R_HEAD
    return jnp.where(r == c, 1.0, 0.0).astype(BF16)


def _rw_gate_kernel(*refs, tm, has_vfirst):
    if has_vfirst:
        (rkv_ref, lh_ref, vf_ref, w2_ref, a2_ref, g2_ref, v2_ref, pv_ref,
         t6_ref, gt_ref, v_ref, aux_ref) = refs
    else:
        rkv_ref, lh_ref, w2_ref, a2_ref, g2_ref, pv_ref, t6_ref, gt_ref, v_ref, aux_ref = refs
    r = rkv_ref[0]
    k = rkv_ref[1]
    v = rkv_ref[2]
    pv = pv_ref[...]
    seg = lambda a, b: lh_ref[:, a:b].astype(BF16)
    if has_vfirst:
        v = v + (vf_ref[...] - v) * _sigmoid(pv[4:5] + _dot(seg(_LH_V, _LH_END), v2_ref[...]))
    v_ref[...] = v
    bd = _head_bd()
    kk = k * pv[5:6]
    kk = kk * lax.rsqrt(jnp.maximum(_head_sums(kk * kk, bd), 1e-24))
    aux_ref[1] = _dot(seg(_LH_G, _LH_V), g2_ref[...]).astype(aux_ref.dtype)
    row = lax.broadcasted_iota(jnp.int32, (tm, tm), 0)
    col = lax.broadcasted_iota(jnp.int32, (tm, tm), 1)
    same = (row // R_CHUNK) == (col // R_CHUNK)
    hw = seg(_LH_W, _LH_A)
    ha = seg(_LH_A, _LH_G)
    ksum = jnp.zeros_like(k)
    nchunk = tm // R_CHUNK
    for d in range(2):
        lw = _sigmoid(pv[d:d + 1] + _dot(hw, w2_ref[d])) * (-_EXP_M05)
        a = _sigmoid(pv[2 + d:3 + d] + _dot(ha, a2_ref[d]))
        kd = k * (1.0 + (a - 1.0) * pv[6:7])
        bv = kk * a
        ksum = ksum + kd
        tri = jnp.where(same & ((col <= row) if d == 0 else (col >= row)), 1.0, 0.0).astype(BF16)
        cum = _dot_01_lhs(tri, lw)
        end = R_CHUNK - 1 if d == 0 else 0
        gt = jnp.exp(cum.reshape(nchunk, R_CHUNK, cum.shape[1])[:, end:end + 1, :])
        e_pos = jnp.exp(cum)
        e_neg = jnp.exp(-cum)
        e_end = (e_neg.reshape(nchunk, R_CHUNK, cum.shape[1]) * gt).reshape(cum.shape)
        t6_ref[d, 0] = (r * e_pos).astype(BF16)
        t6_ref[d, 1] = (-kk * jnp.exp(cum - lw)).astype(BF16)
        t6_ref[d, 2] = (kd * e_neg).astype(BF16)
        t6_ref[d, 3] = (bv * e_neg).astype(BF16)
        t6_ref[d, 4] = (kd * e_end).astype(BF16)
        t6_ref[d, 5] = (bv * e_end).astype(BF16)
        for c in range(nchunk):
            gt_ref[d, c] = gt[c]
    bonus = _head_sums(r * ksum * pv[7:8], bd) * v
    aux_ref[0] = bonus.astype(aux_ref.dtype)


def rw_gates(rkv, lh, v_first, w2, a2, g2, v2, pvec, *, tm=256, tc=1024):
    _, n, d = rkv.shape
    has_vf = v_first is not None
    tile = pl.BlockSpec((tm, tc), lambda i, j: (i, j))
    in_specs = [pl.BlockSpec((3, tm, tc), lambda i, j: (0, i, j)),
                pl.BlockSpec((tm, _LH_END), lambda i, j: (i, 0))]
    args = [rkv, lh]
    if has_vf:
        in_specs.append(tile)
        args.append(v_first)
    in_specs += [pl.BlockSpec((2, 256, tc), lambda i, j: (0, 0, j)),
                 pl.BlockSpec((2, 256, tc), lambda i, j: (0, 0, j)),
                 pl.BlockSpec((256, tc), lambda i, j: (0, j))]
    args += [w2, a2, g2]
    if has_vf:
        in_specs.append(pl.BlockSpec((128, tc), lambda i, j: (0, j)))
        args.append(v2)
    in_specs.append(pl.BlockSpec((8, tc), lambda i, j: (0, j)))
    args.append(pvec)
    nch = tm // R_CHUNK
    return pl.pallas_call(
        functools.partial(_rw_gate_kernel, tm=tm, has_vfirst=has_vf),
        out_shape=[jax.ShapeDtypeStruct((2, 6, n, d), BF16),
                   jax.ShapeDtypeStruct((2, n // R_CHUNK, 1, d), F32),
                   jax.ShapeDtypeStruct((n, d), F32),
                   jax.ShapeDtypeStruct((2, n, d), BF16)],
        grid=(n // tm, d // tc),
        in_specs=in_specs,
        out_specs=[pl.BlockSpec((2, 6, tm, tc), lambda i, j: (0, 0, i, j)),
                   pl.BlockSpec((2, nch, 1, tc), lambda i, j: (0, i, 0, j)),
                   tile,
                   pl.BlockSpec((2, tm, tc), lambda i, j: (0, i, j))],
        compiler_params=_cparams(("parallel", "parallel")),
        name="rw_gates",
    )(*args)


def _pair_stack(y, hi_lane):
    z = jnp.zeros_like(y)
    return jnp.concatenate([jnp.where(hi_lane, z, y), jnp.where(hi_lane, y, z)], axis=0)


def _rw_chunk_kernel(t6_ref, v_ref, gt_ref, y_ref, s_scr, *, npair, unroll):
    d = pl.program_id(1)
    tc = R_CHUNK

    @pl.when(pl.program_id(2) == 0)
    def _():
        s_scr[...] = jnp.zeros_like(s_scr)

    lane = lax.broadcasted_iota(jnp.int32, (tc, 128), 1)
    row = lax.broadcasted_iota(jnp.int32, (tc, 128), 0)
    hi_lane = lane >= R_HEAD
    rel = ((lane & (R_HEAD - 1)) - row) * (1 - 2 * d)
    strict = rel < 0
    incl = rel <= 0
    eye2 = jnp.where(rel == 0, 1.0, 0.0)

    stack = lambda y: _pair_stack(y.astype(BF16), hi_lane)
    rows = lambda a, b: jnp.concatenate([a.astype(BF16), b.astype(BF16)], axis=0)
    top, bot = (lambda m: m[:tc]), (lambda m: m[tc:])
    lft, rgt = (lambda m: m[:, :128]), (lambda m: m[:, 128:])

    def pm(a, y):
        return _dot(a.astype(BF16), stack(y))

    def pm2(a, y1, y2):
        return _dot(a.astype(BF16), jnp.concatenate([stack(y1), stack(y2)], axis=1))

    def pack_kv(full):
        return jnp.where(hi_lane, full[R_HEAD:], full[:R_HEAD])

    def body(it, carry):
        sls = [pl.ds(pl.multiple_of((it * unroll + u) * 128, 128), 128) for u in range(unroll)]
        each = lambda f, *cols: [f(*xs) for xs in zip(*cols)]
        rt, at, kt, bt, kh, bh = ([t6_ref[c, :, sl] for sl in sls] for c in range(6))
        v = [v_ref[:, sl].astype(BF16) for sl in sls]
        ar = each(rows, at, rt)
        xb = each(lambda x, y: _dot_nt(x, stack(y)), ar, bt)
        xk = each(lambda x, y: _dot_nt(x, stack(y)), ar, kt)
        lab = each(lambda m: jnp.where(strict, top(m), 0.0), xb)
        arb = each(lambda m: jnp.where(incl, bot(m), 0.0), xb)
        lak = each(lambda m: jnp.where(strict, top(m), 0.0), xk)
        ark = each(lambda m: jnp.where(incl, bot(m), 0.0), xk)
        inv = each(lambda l: eye2 + l, lab)
        pw = each(pm, lab, lab)
        for _ in range(4):
            both = each(lambda i, p: pm(rows(i, p), p), inv, pw)
            inv = each(lambda i, m: i + top(m), inv, both)
            pw = each(bot, both)
        inv = each(lambda i, p: i + pm(i, p), inv, pw)
        lv = each(lambda a, b, vv: pm(rows(a, b), vv), lak, ark, v)
        m2 = each(top, lv)
        wu = each(pm2, inv, at, m2)
        au = each(lambda a, m: pm2(a, lft(m), rgt(m)), arb, wu)
        q = each(lambda r, m: r.astype(F32) + lft(m), rt, au)
        y0 = each(lambda m, n: bot(m) + rgt(n), lv, au)
        bwu = each(lambda b, m: _dot_tn(b, m.astype(BF16)), bh, wu)
        g = each(lambda sl, m: jnp.where(rel == 0, gt_ref[:, sl], 0.0) + pack_kv(lft(m)), sls, bwu)
        h = each(lambda k, vv, m: pack_kv(_dot_tn(k, vv)) + pack_kv(rgt(m)), kh, v, bwu)
        s_old = [s_scr[:, sl] for sl in sls]
        s_hi = each(lambda s: s.astype(BF16), s_old)
        s_lo = each(lambda s, hi: (s - hi.astype(F32)).astype(BF16), s_old, s_hi)
        gq = each(rows, g, q)
        res = each(lambda m, hi, lo: _dot(m, _pair_stack(hi, hi_lane)) + _dot(m, _pair_stack(lo, hi_lane)),
                   gq, s_hi, s_lo)
        for u, sl in enumerate(sls):
            s_scr[:, sl] = top(res[u]) + h[u]
            y_ref[:, sl] = bot(res[u]) + y0[u]
        return carry

    lax.fori_loop(0, npair // unroll, body, 0)


def rw_chunk_scan(t6, v, gt, *, batch, lc, ctx):
    _, _, n, d = t6.shape
    tc = R_CHUNK
    nc = lc // tc
    nc_ctx = ctx // tc
    rb = lambda b, e, j: b * nc + _chunk_order(e, j, nc, nc_ctx)
    return pl.pallas_call(
        functools.partial(_rw_chunk_kernel, npair=d // 128, unroll=16),
        out_shape=jax.ShapeDtypeStruct((2, n, d), F32),
        grid=(batch, 2, nc),
        in_specs=[pl.BlockSpec((None, 6, tc, d), lambda b, e, j: (e, 0, rb(b, e, j), 0)),
                  pl.BlockSpec((tc, d), lambda b, e, j: (rb(b, e, j), 0)),
                  pl.BlockSpec((None, None, 1, d), lambda b, e, j: (e, rb(b, e, j), 0, 0))],
        out_specs=pl.BlockSpec((None, tc, d), lambda b, e, j: (e, rb(b, e, j), 0)),
        scratch_shapes=[pltpu.VMEM((tc, d), F32)],
        compiler_params=_cparams(("parallel", "parallel", "arbitrary")),
        name="rw_chunk_scan",
    )(t6, v, gt)


def _rw_out_kernel(x_ref, g1_ref, y_ref, aux_ref, ln_ref, wo_ref, out_ref, z_scr, *, tm, tpb, ctx):
    i = pl.program_id(0)
    j = pl.program_id(1)

    @pl.when(j == 0)
    def _():
        bd = _head_bd()
        for c in range(y_ref.shape[2] // _MXU_N):
            sl = slice(c * _MXU_N, (c + 1) * _MXU_N)
            y = y_ref[0, :, sl] + y_ref[1, :, sl]
            mu = _head_sums(y, bd) * (1.0 / R_HEAD)
            yc = y - mu
            var = _head_sums(yc * yc, bd) * (1.0 / R_HEAD)
            z = yc * lax.rsqrt(var + R_LN_EPS) * ln_ref[0:1, sl] + ln_ref[1:2, sl]
            z_scr[:, sl] = ((z + aux_ref[0, :, sl].astype(F32)) * aux_ref[1, :, sl].astype(F32)).astype(BF16)

    b = i // tpb
    is_ctx = ((i % tpb) * tm + _row_iota(tm)) < ctx
    out_ref[...] = x_ref[...] + _mod_rows(g1_ref, b, is_ctx) * _dot(z_scr[...], wo_ref[...])


def rw_out_block(x, mod, y, aux, ln_wb, w_o, *, lc, ctx, tm, tn=512):
    n, d = x.shape
    nj = d // tn
    return pl.pallas_call(
        functools.partial(_rw_out_kernel, tm=tm, tpb=lc // tm, ctx=ctx),
        out_shape=jax.ShapeDtypeStruct((n, d), F32),
        grid=(n // tm, nj),
        in_specs=[pl.BlockSpec((tm, tn), lambda i, j: (i, j)),
                  pl.BlockSpec((8, tn), lambda i, j: (0, 2 * nj + j)),
                  pl.BlockSpec((2, tm, d), lambda i, j: (0, i, 0)),
                  pl.BlockSpec((2, tm, d), lambda i, j: (0, i, 0)),
                  pl.BlockSpec((2, d), lambda i, j: (0, 0)),
                  pl.BlockSpec((d, tn), lambda i, j: (0, j))],
        out_specs=pl.BlockSpec((tm, tn), lambda i, j: (i, j)),
        scratch_shapes=[pltpu.VMEM((tm, d), BF16)],
        compiler_params=_cparams(("parallel", "arbitrary")),
        name="rw_out",
    )(x, mod, y, aux, ln_wb, w_o)


def odd_params(mu, w_r, w_k, w_v, w_o, w0, w1, w2, a0, a1, a2, g1, g2, k_k, k_a, r_k, ln_w, ln_b,
               v0=None, v1=None, v2=None):
    d = w_r.shape[0]
    dw, da, dg = w1.shape[-1], a1.shape[-1], g1.shape[-1]
    assert 2 * dw <= _LH_A - _LH_W and 2 * da <= _LH_G - _LH_A and dg <= _LH_V - _LH_G
    wl = jnp.zeros((d, _LH_END), F32)
    wl = wl.at[:, _LH_W:_LH_W + dw].set(w1[0]).at[:, _LH_W + dw:_LH_W + 2 * dw].set(w1[1])
    wl = wl.at[:, _LH_A:_LH_A + da].set(a1[0]).at[:, _LH_A + da:_LH_A + 2 * da].set(a1[1])
    wl = wl.at[:, _LH_G:_LH_G + dg].set(g1)
    w2p = jnp.zeros((2, 256, d), F32).at[0, :dw].set(w2[0]).at[1, dw:2 * dw].set(w2[1])
    a2p = jnp.zeros((2, 256, d), F32).at[0, :da].set(a2[0]).at[1, da:2 * da].set(a2[1])
    g2p = jnp.zeros((256, d), F32).at[:dg].set(g2)
    v2p = None
    vzero = jnp.zeros((d,), F32)
    if v1 is not None:
        dv = v1.shape[-1]
        assert dv <= _LH_END - _LH_V
        wl = wl.at[:, _LH_V:_LH_V + dv].set(v1)
        v2p = jnp.zeros((128, d), F32).at[:dv].set(v2).astype(BF16)
    pvec = jnp.stack([w0[0], w0[1], a0[0], a0[1], v0 if v0 is not None else vzero,
                      k_k, k_a, r_k.reshape(d)])
    return dict(mu=mu, w_lora1=wl.astype(BF16), w_rkv=jnp.stack([w_r, w_k, w_v]).astype(BF16),
                w2=w2p.astype(BF16), a2=a2p.astype(BF16), g2=g2p.astype(BF16), v2=v2p, pvec=pvec,
                ln_wb=jnp.stack([ln_w, ln_b]), w_o=w_o.astype(BF16))


def odd_layer(x, mod, nw, p, v_first, *, batch, lc, ctx, tm):
    rkv, lh = rw_project(x, mod, nw, p["mu"], p["w_lora1"], p["w_rkv"], lc=lc, ctx=ctx)
    t6, gt, v, aux = rw_gates(rkv, lh, v_first if p["v2"] is not None else None,
                              p["w2"], p["a2"], p["g2"], p["v2"], p["pvec"])
    y = rw_chunk_scan(t6, v, gt, batch=batch, lc=lc, ctx=ctx)
    return rw_out_block(x, mod, y, aux, p["ln_wb"], p["w_o"], lc=lc, ctx=ctx, tm=tm), v


def _row_tile(lc, cap):
    return max(t for t in range(16, cap + 1, 16) if lc % t == 0)


def kernel(x, c, ctx, c_ctx, ada_w, ada_b, norm_mix, norm_ffn, ffn_w_up, ffn_conv_w, ffn_conv_b, ffn_w_down, norm_final, ev_w_in, ev_b_in, ev_w_out, s5_lam_re, s5_lam_im, s5_log_step, s5_b_re, s5_b_im, s5_c_re, s5_c_im, s5_d, s5_w_glu, s5_b_glu, ml_conv_w, ml_conv_b, ml_norm, rw_mu, rw_w_r, rw_w_k, rw_w_v, rw_w_o, rw_w0, rw_w1, rw_w2, rw_a0, rw_a1, rw_a2, rw_v0, rw_v1, rw_v2, rw_g1, rw_g2, rw_k_k, rw_k_a, rw_r_k, rw_ln_w, rw_ln_b):
    batch, seq, d = x.shape
    n_ctx = ctx.shape[1]
    lc = n_ctx + seq
    depth = ada_w.shape[0]
    n_even = ev_w_in.shape[0]
    assert batch < _CTX_ROW and n_ctx % 256 == 0 and seq % 256 == 0 and seq % GRID_W == 0
    tm = _row_tile(lc, 544)
    tm_small = _row_tile(lc, 272)

    xc = jnp.concatenate([ctx, x], axis=1).reshape(batch * lc, d)
    c8 = jnp.zeros((8, d), F32).at[:batch].set(c).at[_CTX_ROW].set(c_ctx)
    mods = adaln(c8, ada_w, ada_b)

    g, p_state = s5_lam_re.shape[-2:]
    flat = lambda a: a.reshape((n_even * 2,) + a.shape[2:])
    s5p = s5_params(flat(s5_lam_re), flat(s5_lam_im), s5_log_step.reshape(n_even * 2, g),
                    flat(s5_b_re), flat(s5_b_im), flat(s5_c_re), flat(s5_c_im))

    v_first = None
    for l in range(depth):
        j = l // 2
        if l % 2 == 0:
            ep = even_params(ev_w_in[j], ev_b_in[j], ev_w_out[j], s5_w_glu[j], s5_b_glu[j],
                             ml_conv_w[j], ml_conv_b[j], ml_norm[j])
            ops = s5_assemble(tuple(a[2 * j:2 * j + 2] for a in s5p), s5_d[j])
            xc = even_layer(xc, mods[l], norm_mix[l], ep, ops, batch=batch, lc=lc, ctx=n_ctx, tm=tm)
        else:
            extra = {} if j == 0 else dict(v0=rw_v0[j - 1], v1=rw_v1[j - 1], v2=rw_v2[j - 1])
            op = odd_params(rw_mu[j], rw_w_r[j], rw_w_k[j], rw_w_v[j], rw_w_o[j], rw_w0[j], rw_w1[j],
                            rw_w2[j], rw_a0[j], rw_a1[j], rw_a2[j], rw_g1[j], rw_g2[j], rw_k_k[j],
                            rw_k_a[j], rw_r_k[j], rw_ln_w[j], rw_ln_b[j], **extra)
            xc, v = odd_layer(xc, mods[l], norm_mix[l], op, v_first, batch=batch, lc=lc, ctx=n_ctx,
                              tm=tm_small)
            if j == 0:
                v_first = v
        xc = conv_ffn_block(xc, mods[l], norm_ffn[l], ffn_w_up[l].astype(BF16), ffn_conv_w[l],
                            ffn_conv_b[l], ffn_w_down[l].astype(BF16), lc=lc, ctx=n_ctx, tm=tm)
    out = final_norm(xc, norm_final, batch=batch, lc=lc, ctx=n_ctx)
    return out.reshape(batch, seq, d)
```

```python
import functools

import jax
import jax.numpy as jnp
from jax import lax
from jax.experimental import pallas as pl
from jax.experimental.pallas import tpu as pltpu

F32 = jnp.float32
BF16 = jnp.bfloat16
EPS = 1e-6
NEG_INF = -1e30
GRID_W = 64
S5_T = 16
M_HEADS = 4
M_CHUNK = 128
R_HEAD = 64
R_CHUNK = 64
R_LN_EPS = 64e-5
HIGHEST = lax.Precision.HIGHEST
VMEM_LIMIT = 56 * 1024 * 1024
_MXU_N = 256


def _cparams(sem):
    return pltpu.CompilerParams(dimension_semantics=sem, vmem_limit_bytes=VMEM_LIMIT)


def _dot(a, b):
    return jnp.dot(a, b, preferred_element_type=F32)


def _dot_nt(a, b):
    return lax.dot_general(a, b, (((1,), (1,)), ((), ())), preferred_element_type=F32)


def _dot_tn(a, b):
    return lax.dot_general(a, b, (((0,), (0,)), ((), ())), preferred_element_type=F32)


def _split3(x):
    h = x.astype(BF16)
    r = x - h.astype(F32)
    m = r.astype(BF16)
    l = (r - m.astype(F32)).astype(BF16)
    return h, m, l


def _dot_exact_lhs(a_bf16, x):
    h, m, l = _split3(x)
    return _dot(a_bf16, h) + _dot(a_bf16, m) + _dot(a_bf16, l)


def _dot_01_lhs(a_bf16, x):
    h = x.astype(BF16)
    l = (x - h.astype(F32)).astype(BF16)
    return _dot(a_bf16, h) + _dot(a_bf16, l)


_EXP_M05 = 0.6065306597126334


def _sigmoid(x):
    return jax.nn.sigmoid(x)


def _silu(x):
    return x * jax.nn.sigmoid(x)


def _gelu_tanh(x):
    return 0.5 * x * (1.0 + jnp.tanh(0.7978845608028654 * (x + 0.044715 * (x * x * x))))


def _row_iota(n):
    return lax.broadcasted_iota(jnp.int32, (n, 1), 0)


def _mod_rows(ref, b, is_ctx):
    lat = ref[pl.ds(b, 1), :]
    ctx = ref[pl.ds(_CTX_ROW, 1), :]
    return jnp.where(is_ctx, ctx, lat)


_CTX_ROW = 7


def _norm_mod(x, nw, sc, sh):
    ms = jnp.mean(x * x, axis=-1, keepdims=True)
    return x * lax.rsqrt(ms + EPS) * nw * (1.0 + sc) + sh


def _adaln_kernel(c_ref, w_ref, b_ref, o_ref):
    a = _silu(c_ref[...])
    o_ref[0] = _dot(a.astype(BF16), w_ref[0].astype(BF16)) + b_ref[0]


def adaln(c8, ada_w, ada_b, tn=1024):
    depth, d, n6 = ada_w.shape
    return pl.pallas_call(
        _adaln_kernel,
        out_shape=jax.ShapeDtypeStruct((depth, 8, n6), F32),
        grid=(depth, n6 // tn),
        in_specs=[pl.BlockSpec((8, d), lambda l, j: (0, 0)),
                  pl.BlockSpec((1, d, tn), lambda l, j: (l, 0, j)),
                  pl.BlockSpec((1, 1, tn), lambda l, j: (l, 0, j))],
        out_specs=pl.BlockSpec((1, 8, tn), lambda l, j: (l, 0, j)),
        compiler_params=_cparams(("parallel", "parallel")),
        name="adaln",
    )(c8, ada_w, ada_b.reshape(depth, 1, n6))


def _halo_specs(tm, d, n_rows, halo):
    r = tm // halo
    nblk = n_rows // halo
    return [pl.BlockSpec((tm, d), lambda i, j: (i, 0)),
            pl.BlockSpec((halo, d), lambda i, j: (jnp.maximum(i * r - 1, 0), 0)),
            pl.BlockSpec((halo, d), lambda i, j: (jnp.minimum(i * r + r, nblk - 1), 0))]


def _mod_spec(d, k):
    return pl.BlockSpec((8, d), lambda i, j: (0, k))


def _conv3(u, first, last, cw, cb):
    n = u.shape[0]
    up = jnp.where(first, 0.0, pltpu.roll(u, 1, 0))
    un = jnp.where(last, 0.0, pltpu.roll(u, n - 1, 0))
    return up * cw[0:1] + u * cw[1:2] + un * cw[2:3] + cb


def _ext_rows(xm_ref, xp_ref, xn_ref, sh_ref, sc_ref, nw_ref, i, tm, tpb, ctx, halo):
    b = i // tpb
    pos = (i % tpb) * tm - halo + _row_iota(tm + 2 * halo)
    is_ctx = pos < ctx
    xe = jnp.concatenate([xp_ref[...], xm_ref[...], xn_ref[...]], axis=0)
    h = _norm_mod(xe, nw_ref[...], _mod_rows(sc_ref, b, is_ctx), _mod_rows(sh_ref, b, is_ctx))
    return h, pos


def _ffn_kernel(xm_ref, xp_ref, xn_ref, sh_ref, sc_ref, g_ref, nw_ref, wa_ref, wg_ref,
                cp_ref, wd_ref, o_ref, h_scr, act_scr, acc_scr, *, tm, tn, nj, tpb, ctx, lc):
    i = pl.program_id(0)
    j = pl.program_id(1)

    hm = tm // 2
    wins = (0, hm)

    def up_dots():
        out = []
        for r0 in wins:
            h = h_scr[r0:r0 + hm + 16, :]
            out.append((_dot(h, wa_ref[...]), _dot(h, wg_ref[...])))
        return out

    def down_prev():
        acc_scr[...] += _dot(act_scr[(j + 1) % 2], wd_ref[...])

    def activate(us):
        ca = pl.ds(pl.multiple_of(j * tn, tn), tn)
        cg = pl.ds(pl.multiple_of((j + nj) * tn, tn), tn)
        for r0, (ua, ug) in zip(wins, us):
            pos = (i % tpb) * tm + r0 - 8 + _row_iota(hm + 16)
            first = (pos == 0) | (pos == ctx)
            last = (pos == ctx - 1) | (pos == lc - 1)
            a = _conv3(ua, first, last, cp_ref[0:3, ca], cp_ref[3:4, ca])[8:8 + hm]
            g = _conv3(ug, first, last, cp_ref[0:3, cg], cp_ref[3:4, cg])[8:8 + hm]
            act_scr[j % 2, r0:r0 + hm, :] = (a * _silu(g)).astype(BF16)

    @pl.when(j == 0)
    def _():
        h, _ = _ext_rows(xm_ref, xp_ref, xn_ref, sh_ref, sc_ref, nw_ref, i, tm, tpb, ctx, 8)
        h_scr[...] = h.astype(BF16)
        acc_scr[...] = jnp.zeros_like(acc_scr)
        activate(up_dots())

    @pl.when((j > 0) & (j < nj))
    def _():
        us = up_dots()
        down_prev()
        activate(us)

    @pl.when(j == nj)
    def _():
        down_prev()
        b = i // tpb
        is_ctx = ((i % tpb) * tm + _row_iota(tm)) < ctx
        o_ref[...] = xm_ref[...] + _mod_rows(g_ref, b, is_ctx) * acc_scr[...]


def conv_ffn_block(x, mod, nw, w_up, conv_w, conv_b, w_down, *, lc, ctx, tm, tn=512):
    n, d = x.shape
    dff = w_down.shape[0]
    nj = dff // tn
    up = lambda j: jnp.minimum(j, nj - 1)
    dn = lambda j: jnp.maximum(j - 1, 0)
    kern = functools.partial(_ffn_kernel, tm=tm, tn=tn, nj=nj, tpb=lc // tm, ctx=ctx, lc=lc)
    cpar = jnp.concatenate([conv_w, conv_b.reshape(1, -1)], axis=0)
    return pl.pallas_call(
        kern,
        out_shape=jax.ShapeDtypeStruct((n, d), F32),
        grid=(n // tm, nj + 1),
        in_specs=_halo_specs(tm, d, n, 8) + [
            _mod_spec(d, 3), _mod_spec(d, 4), _mod_spec(d, 5),
            pl.BlockSpec((1, d), lambda i, j: (0, 0)),
            pl.BlockSpec((d, tn), lambda i, j: (0, up(j))),
            pl.BlockSpec((d, tn), lambda i, j: (0, up(j) + nj)),
            pl.BlockSpec((4, 2 * dff), lambda i, j: (0, 0)),
            pl.BlockSpec((tn, d), lambda i, j: (dn(j), 0)),
        ],
        out_specs=pl.BlockSpec((tm, d), lambda i, j: (i, 0)),
        scratch_shapes=[pltpu.VMEM((tm + 16, d), BF16), pltpu.VMEM((2, tm, tn), BF16),
                        pltpu.VMEM((tm, d), F32)],
        compiler_params=_cparams(("parallel", "arbitrary")),
        name="conv_ffn",
    )(x, x, x, mod, mod, mod, nw.reshape(1, d), w_up, w_up, cpar, w_down)


def _final_norm_kernel(x_ref, w_ref, o_ref):
    x = x_ref[...]
    o_ref[...] = x * lax.rsqrt(jnp.mean(x * x, axis=-1, keepdims=True) + EPS) * w_ref[...]


def final_norm(x, w, *, batch, lc, ctx, tm=256):
    n, d = x.shape
    tpb = lc // tm
    skip = ctx // tm
    per = tpb - skip
    return pl.pallas_call(
        _final_norm_kernel,
        out_shape=jax.ShapeDtypeStruct((batch * per * tm, d), F32),
        grid=(batch, per),
        in_specs=[pl.BlockSpec((tm, d), lambda b, t: (b * tpb + skip + t, 0)),
                  pl.BlockSpec((1, d), lambda b, t: (0, 0))],
        out_specs=pl.BlockSpec((tm, d), lambda b, t: (b * per + t, 0)),
        compiler_params=_cparams(("parallel", "parallel")),
        name="final_norm",
    )(x, w.reshape(1, d))


def _even_in_kernel(xm_ref, xp_ref, xn_ref, sh_ref, sc_ref, nw_ref, wq_ref, bq_ref, cw_ref, cb_ref,
                    s_ref, wu_ref, bu_ref, wg_ref, bg_ref, qk_ref, uvo_ref, g_ref, h_scr,
                    *, tm, tpb, ctx, lc, nq, nu):
    i = pl.program_id(0)
    j = pl.program_id(1)

    @pl.when(j == 0)
    def _():
        h, _ = _ext_rows(xm_ref, xp_ref, xn_ref, sh_ref, sc_ref, nw_ref, i, tm, tpb, ctx, 8)
        h_scr[...] = h.astype(BF16)

    @pl.when(j < nq)
    def _():
        pos = (i % tpb) * tm - 8 + _row_iota(tm + 16)
        first = (pos == 0) | (pos == ctx)
        last = (pos == ctx - 1) | (pos == lc - 1)
        u = _dot(h_scr[...], wq_ref[...]) + bq_ref[...]
        u = _silu(_conv3(u, first, last, cw_ref[...], cb_ref[...])[8:8 + tm]) * s_ref[...]
        qk_ref[...] = u.astype(qk_ref.dtype)

    @pl.when((j >= nq) & (j < nq + nu))
    def _():
        uvo_ref[...] = (_dot(h_scr[8:8 + tm, :], wu_ref[...]) + bu_ref[...]).astype(uvo_ref.dtype)

    @pl.when(j == nq + nu)
    def _():
        g_ref[...] = _dot(h_scr[8:8 + tm, :], wg_ref[...]) + bg_ref[...]


def even_in_proj(x, mod, nw, p, *, lc, ctx, tm, tn=1024):
    n, d = x.shape
    nq = p["w_qk"].shape[1] // tn
    nu = p["w_uvo"].shape[1] // tn
    cq = lambda j: jnp.minimum(j, nq - 1)
    cu = lambda j: jnp.clip(j - nq, 0, nu - 1)
    colq = lambda r: pl.BlockSpec((r, tn), lambda i, j: (0, cq(j)))
    row = lambda a: a.reshape(1, -1)
    return pl.pallas_call(
        functools.partial(_even_in_kernel, tm=tm, tpb=lc // tm, ctx=ctx, lc=lc, nq=nq, nu=nu),
        out_shape=[jax.ShapeDtypeStruct((n, nq * tn), BF16), jax.ShapeDtypeStruct((n, nu * tn), BF16),
                   jax.ShapeDtypeStruct((n, 128), F32)],
        grid=(n // tm, nq + nu + 1),
        in_specs=_halo_specs(tm, d, n, 8) + [
            _mod_spec(d, 0), _mod_spec(d, 1), pl.BlockSpec((1, d), lambda i, j: (0, 0)),
            pl.BlockSpec((d, tn), lambda i, j: (0, cq(j))), colq(1), colq(3), colq(1), colq(1),
            pl.BlockSpec((d, tn), lambda i, j: (0, cu(j))),
            pl.BlockSpec((1, tn), lambda i, j: (0, cu(j))),
            pl.BlockSpec((d, 128), lambda i, j: (0, 0)),
            pl.BlockSpec((1, 128), lambda i, j: (0, 0))],
        out_specs=[pl.BlockSpec((tm, tn), lambda i, j: (i, cq(j))),
                   pl.BlockSpec((tm, tn), lambda i, j: (i, cu(j))),
                   pl.BlockSpec((tm, 128), lambda i, j: (i, 0))],
        scratch_shapes=[pltpu.VMEM((tm + 16, d), BF16)],
        compiler_params=_cparams(("parallel", "arbitrary")),
        name="even_in_proj",
    )(x, x, x, mod, mod, nw.reshape(1, d), p["w_qk"], row(p["b_qk"]), p["conv_w"], row(p["conv_b"]),
      row(p["qk_scale"]), p["w_uvo"], row(p["b_uvo"]), p["w_gate"], row(p["b_gate"]))


def _s5_param_kernel(lr_ref, li_ref, ls_ref, bbr_ref, bbi_ref, cr_ref, ci_ref,
                     wre_ref, wim_ref, ere_ref, eim_ref, k_ref, lam_re_ref, lam_im_ref, *, t, ch):
    step = jnp.exp(ls_ref[0])
    lr = lr_ref[0]
    li = li_ref[0]
    p = lr.shape[-1]
    sr = lr * step
    si = li * step

    def power(tau):
        mag = jnp.exp(tau * sr)
        return mag * jnp.cos(tau * si), mag * jnp.sin(tau * si)

    ab_re, ab_im = power(1.0)
    den = lr * lr + li * li
    co_re = ((ab_re - 1.0) * lr + ab_im * li) / den
    co_im = (ab_im * lr - (ab_re - 1.0) * li) / den
    b_re = bbr_ref[0]
    b_im = bbi_ref[0]
    bb_re = co_re * b_re - co_im * b_im
    bb_im = co_re * b_im + co_im * b_re
    tile = lambda m: jnp.concatenate([m] * t, axis=0)
    tau = (lax.broadcasted_iota(jnp.int32, (t * ch, p), 0) // ch).astype(F32)
    pr, pi = power(tau)
    bbr_t, bbi_t = tile(bb_re), tile(bb_im)
    w_re = pr * bbr_t - pi * bbi_t
    w_im = pr * bbi_t + pi * bbr_t
    wre_ref[0] = w_re
    wim_ref[0] = w_im
    qr, qi = pr * ab_re - pi * ab_im, pr * ab_im + pi * ab_re
    c_re = cr_ref[0]
    c_im = ci_ref[0]
    cr_t, ci_t = tile(c_re), tile(c_im)
    ere_ref[0] = cr_t * qr - ci_t * qi
    eim_ref[0] = -(cr_t * qi + ci_t * qr)
    k_ref[0] = (lax.dot_general(c_re, w_re, (((1,), (1,)), ((), ())), precision=HIGHEST,
                                preferred_element_type=F32)
                - lax.dot_general(c_im, w_im, (((1,), (1,)), ((), ())), precision=HIGHEST,
                                  preferred_element_type=F32))
    lt_re, lt_im = power(float(t))
    lam_re_ref[0] = lt_re
    lam_im_ref[0] = lt_im


def s5_params(lam_re, lam_im, log_step, b_re, b_im, c_re, c_im, t=S5_T):
    nd, g, p = lam_re.shape
    ch = c_re.shape[2]
    m = nd * g
    r3 = lambda a: a.reshape(m, 1, p)
    bt = lambda a: jnp.swapaxes(a, -1, -2).reshape(m, ch, p)
    vec = pl.BlockSpec((1, 1, p), lambda i: (i, 0, 0))
    mat = pl.BlockSpec((1, ch, p), lambda i: (i, 0, 0))
    big = pl.BlockSpec((1, t * ch, p), lambda i: (i, 0, 0))
    outs = pl.pallas_call(
        functools.partial(_s5_param_kernel, t=t, ch=ch),
        out_shape=[jax.ShapeDtypeStruct((m, t * ch, p), F32)] * 4
        + [jax.ShapeDtypeStruct((m, ch, t * ch), F32)] + [jax.ShapeDtypeStruct((m, 1, p), F32)] * 2,
        grid=(m,),
        in_specs=[vec, vec, pl.BlockSpec((1, 1, 1), lambda i: (i, 0, 0)), mat, mat, mat, mat],
        out_specs=[big] * 4 + [pl.BlockSpec((1, ch, t * ch), lambda i: (i, 0, 0)), vec, vec],
        compiler_params=_cparams(("parallel",)),
        name="s5_params",
    )(r3(lam_re), r3(lam_im), log_step.reshape(m, 1, 1), bt(b_re), bt(b_im),
      c_re.reshape(m, ch, p), c_im.reshape(m, ch, p))
    w_re, w_im, e_re, e_im, k, l_re, l_im = outs
    sh = lambda a: a.reshape(nd, g, t, ch, p)
    return (sh(w_re), sh(w_im), sh(e_re), sh(e_im), k.reshape(nd, g, ch, t, ch),
            l_re.reshape(nd, g, p), l_im.reshape(nd, g, p))


def s5_assemble(params, d_skip, t=S5_T):
    w_re, w_im, e_re, e_im, k, l_re, l_im = params
    nd, g, _, ch, p = w_re.shape
    tc = t * ch
    s_i = jnp.arange(t)[:, None]
    t_i = jnp.arange(t)[None, :]
    kf = jnp.take(k[0], jnp.clip(t_i - s_i, 0, t - 1), axis=2)
    kb = jnp.take(k[1], jnp.clip(s_i - t_i, 0, t - 1), axis=2)
    mf = jnp.where((t_i >= s_i)[None, None, :, :, None], kf, 0.0)
    mb = jnp.where((s_i >= t_i)[None, None, :, :, None], kb, 0.0)
    m = jnp.transpose(mf + mb, (0, 2, 4, 3, 1))
    eye = (jnp.eye(t)[:, None, :, None] * jnp.eye(ch)[None, :, None, :])
    m = m + eye[None] * d_skip.reshape(g, 1, 1, 1, ch)
    m = m.reshape(g, tc, tc)
    def fmat(wr, wi):
        wr = wr.reshape(g, tc, p)
        wi = wi.reshape(g, tc, p)
        return jnp.concatenate([wr, wi, wi, wr], axis=-1)
    f2 = jnp.concatenate([fmat(w_re[0][:, ::-1], w_im[0][:, ::-1]), fmat(w_re[1], w_im[1])], axis=-1)
    def emat(er, ei):
        return jnp.concatenate([jnp.swapaxes(er.reshape(g, tc, p), 1, 2),
                                jnp.swapaxes(ei.reshape(g, tc, p), 1, 2)], axis=1)
    e2 = jnp.concatenate([emat(e_re[0], e_im[0]), emat(e_re[1][:, ::-1], e_im[1][:, ::-1])], axis=1)
    def coef(lr, li):
        a = jnp.concatenate([lr, lr], axis=-1)
        b1 = jnp.concatenate([-li, li], axis=-1)
        b2 = jnp.concatenate([li, -li], axis=-1)
        return jnp.concatenate([a, a], axis=-1), jnp.concatenate([b1, b2], axis=-1)
    af, bf = coef(l_re[0], l_im[0])
    ab, bb = coef(l_re[1], l_im[1])
    coefs = jnp.stack([jnp.concatenate([af, ab], axis=-1), jnp.concatenate([bf, bb], axis=-1)], axis=1)
    return m.astype(BF16), f2.astype(BF16), e2.astype(BF16), coefs


def _s5_in_kernel(u_ref, f_ref, z_ref):
    z_ref[...] = _dot(u_ref[0].astype(BF16), f_ref[0])


def s5_chunk_inputs(ut, f2):
    g, r, tc = ut.shape
    w = f2.shape[-1]
    return pl.pallas_call(
        _s5_in_kernel,
        out_shape=jax.ShapeDtypeStruct((r, g * w), F32),
        grid=(g,),
        in_specs=[pl.BlockSpec((1, r, tc), lambda i: (i, 0, 0)),
                  pl.BlockSpec((1, tc, w), lambda i: (i, 0, 0))],
        out_specs=pl.BlockSpec((r, w), lambda i: (0, i)),
        compiler_params=_cparams(("parallel",)),
        name="s5_chunk_inputs",
    )(ut, f2)


def _s5_scan_kernel(z_ref, c_ref, h_ref, *, gb, nc, nc_ctx, lw):
    nb = z_ref.shape[0]
    ng, ng_ctx = nc // 8, nc_ctx // 8

    def body(jg, carry):
        rows = (pl.multiple_of(jg * 8, 8),
                pl.multiple_of(_chunk_order(1, jg, ng, ng_ctx) * 8, 8))
        new = list(carry)
        for gi in range(gb):
            for di in range(2):
                idx = 2 * (2 * gi + di)
                hs, hx = new[idx], new[idx + 1]
                base = (gi * 4 + 2 * di) * lw
                z8 = z_ref[:, pl.ds(rows[di], 8), base:base + lw]
                zx8 = z_ref[:, pl.ds(rows[di], 8), base + lw:base + 2 * lw]
                a = c_ref[gi, 0:1, 2 * di * lw:(2 * di + 1) * lw]
                b1 = c_ref[gi, 1:2, 2 * di * lw:(2 * di + 1) * lw]
                b2 = c_ref[gi, 1:2, (2 * di + 1) * lw:(2 * di + 2) * lw]
                entry = [None] * 8
                for s in range(8):
                    r = s if di == 0 else 7 - s
                    entry[r] = hs
                    hs, hx = (a * hs + b1 * hx + z8[:, r:r + 1, :],
                              a * hx + b2 * hs + zx8[:, r:r + 1, :])
                new[idx], new[idx + 1] = hs, hx
                h_ref[:, pl.ds(rows[di], 8), (gi * 2 + di) * lw:(gi * 2 + di + 1) * lw] = (
                    jnp.concatenate(entry, axis=1))
        return tuple(new)

    init = tuple(jnp.zeros((nb, 1, lw), F32) for _ in range(4 * gb))
    lax.fori_loop(0, ng, body, init)


def s5_chunk_scan(z, coefs, *, batch, nc, nc_ctx, gb=2):
    r, wtot = z.shape
    g = coefs.shape[0]
    lw = wtot // g // 4
    assert nc % 8 == 0 and nc_ctx % 8 == 0
    z3 = z.reshape(batch, nc, wtot)
    return pl.pallas_call(
        functools.partial(_s5_scan_kernel, gb=gb, nc=nc, nc_ctx=nc_ctx, lw=lw),
        out_shape=jax.ShapeDtypeStruct((batch, nc, g * 2 * lw), F32),
        grid=(g // gb,),
        in_specs=[pl.BlockSpec((batch, nc, gb * 4 * lw), lambda i: (0, 0, i)),
                  pl.BlockSpec((gb, 2, 4 * lw), lambda i: (i, 0, 0))],
        out_specs=pl.BlockSpec((batch, nc, gb * 2 * lw), lambda i: (0, 0, i)),
        compiler_params=_cparams(("parallel",)),
        name="s5_chunk_scan",
    )(z3, coefs).reshape(r, g * 2 * lw)


def _s5_out_kernel(u_ref, m_ref, h_ref, e_ref, y_ref):
    y = _dot(u_ref[0].astype(BF16), m_ref[0]) + _dot(h_ref[...].astype(BF16), e_ref[0])
    y_ref[0] = y.astype(y_ref.dtype)


def s5_chunk_outputs(ut, m, hs, e2):
    g, r, tc = ut.shape
    hw = e2.shape[1]
    return pl.pallas_call(
        _s5_out_kernel,
        out_shape=jax.ShapeDtypeStruct((g, r, tc), F32),
        grid=(g,),
        in_specs=[pl.BlockSpec((1, r, tc), lambda i: (i, 0, 0)),
                  pl.BlockSpec((1, tc, tc), lambda i: (i, 0, 0)),
                  pl.BlockSpec((r, hw), lambda i: (0, i)),
                  pl.BlockSpec((1, hw, tc), lambda i: (i, 0, 0))],
        out_specs=pl.BlockSpec((1, r, tc), lambda i: (i, 0, 0)),
        compiler_params=_cparams(("parallel",)),
        name="s5_chunk_outputs",
    )(ut, m, hs, e2)


def _lane_piece(src, src_off, dst_off, width, acc, lane):
    moved = pltpu.roll(src, (dst_off - src_off) % 128, 1) if dst_off != src_off else src
    if acc is None:
        return moved
    return jnp.where(lane // width == dst_off // width, moved, acc)


def _to_chunks_kernel(u_ref, o_ref, scr, *, t, ch):
    rt = o_ref.shape[1]
    g = o_ref.shape[0]
    per = 128 // ch
    nblk = scr.shape[0]
    for m in range(nblk):
        scr[m] = u_ref[:, 128 * m:128 * (m + 1)].astype(F32)
    lane = lax.broadcasted_iota(jnp.int32, (rt, 128), 1)
    for m in range(nblk):
        rows = [scr[m, pl.ds(tt, rt, stride=t), :] for tt in range(t)]
        for gi in range(m * per, (m + 1) * per):
            so = ch * (gi % per)
            for h in range(t * ch // 128):
                acc = None
                for k in range(per):
                    acc = _lane_piece(rows[h * per + k], so, ch * k, ch, acc, lane)
                o_ref[gi, :, 128 * h:128 * (h + 1)] = acc


def _from_chunks_kernel(y_ref, o_ref, scr, *, t, ch):
    g, rt, _ = y_ref.shape
    per = 128 // ch
    lane = lax.broadcasted_iota(jnp.int32, (rt, 128), 1)
    for m in range(g // per):
        for tt in range(t):
            h, so = tt // per, ch * (tt % per)
            acc = None
            for k in range(per):
                acc = _lane_piece(y_ref[m * per + k, :, 128 * h:128 * (h + 1)], so, ch * k, ch, acc, lane)
            scr[m, pl.ds(tt, rt, stride=t), :] = acc
        o_ref[:, 128 * m:128 * (m + 1)] = scr[m].astype(o_ref.dtype)


def s5_to_chunks(uvo, g, *, t=S5_T, rt=136):
    n = uvo.shape[0]
    w = uvo.shape[1] // 3
    ch = w // g
    r = n // t
    return pl.pallas_call(
        functools.partial(_to_chunks_kernel, t=t, ch=ch),
        out_shape=jax.ShapeDtypeStruct((g, r, t * ch), F32),
        grid=(r // rt,),
        in_specs=[pl.BlockSpec((rt * t, w), lambda i: (i, 0))],
        out_specs=pl.BlockSpec((g, rt, t * ch), lambda i: (0, i, 0)),
        scratch_shapes=[pltpu.VMEM((w // 128, rt * t, 128), F32)],
        compiler_params=_cparams(("parallel",)),
        name="s5_to_chunks",
    )(uvo)


def s5_from_chunks(yt, *, t=S5_T, rt=136):
    g, r, tc = yt.shape
    ch = tc // t
    w = g * ch
    return pl.pallas_call(
        functools.partial(_from_chunks_kernel, t=t, ch=ch),
        out_shape=jax.ShapeDtypeStruct((r * t, w), BF16),
        grid=(r // rt,),
        in_specs=[pl.BlockSpec((g, rt, tc), lambda i: (0, i, 0))],
        out_specs=pl.BlockSpec((rt * t, w), lambda i: (i, 0)),
        scratch_shapes=[pltpu.VMEM((w // 128, rt * t, 128), F32)],
        compiler_params=_cparams(("parallel",)),
        name="s5_from_chunks",
    )(yt)


def s5_mix(uvo, ops, *, batch, lc, ctx, t=S5_T):
    m, f2, e2, coefs = ops
    n = uvo.shape[0]
    g = m.shape[0]
    rt = next(c for c in (136, 128, 64, 32, 16, 8) if (n // t) % c == 0)
    ut = s5_to_chunks(uvo, g, t=t, rt=rt)
    z = s5_chunk_inputs(ut, f2)
    hs = s5_chunk_scan(z, coefs, batch=batch, nc=lc // t, nc_ctx=ctx // t)
    yt = s5_chunk_outputs(ut, m, hs, e2)
    return s5_from_chunks(yt, t=t, rt=rt)


def _chunk_order(d, j, nc, nc_ctx):
    bwd = jnp.where(j < nc_ctx, nc_ctx - 1 - j, nc - 1 - (j - nc_ctx))
    return jnp.where(d == 0, j, bwd)


def _mlstm_kernel(qkf_ref, vf_ref, gf_ref, qkb_ref, vb_ref, gb_ref, of_ref, ob_ref,
                  c_scr, n_scr, m_scr, *, nh):
    j = pl.program_id(1)
    tc = qkf_ref.shape[0]
    mw = vf_ref.shape[1]
    dh = mw // nh

    @pl.when(j == 0)
    def _():
        c_scr[...] = jnp.zeros_like(c_scr)
        n_scr[...] = jnp.zeros_like(n_scr)
        m_scr[...] = jnp.full_like(m_scr, NEG_INF)

    row = lax.broadcasted_iota(jnp.int32, (tc, tc), 0)
    col = lax.broadcasted_iota(jnp.int32, (tc, tc), 1)
    lane = lax.broadcasted_iota(jnp.int32, (tc, 128), 1)
    chains = [(d, hd) for d in range(2) for hd in range(nh)]
    each = lambda f, *cols: [f(*xs) for xs in zip(*cols)]
    qk_refs, v_refs, g_vals = (qkf_ref, qkb_ref), (vf_ref, vb_ref), (gf_ref[...], gb_ref[...])
    vis_d = [col <= row, col >= row]
    tri_d = [jnp.where(m, 1.0, 0.0).astype(BF16) for m in vis_d]
    pick = lambda d, c: jnp.sum(jnp.where(lane == c, g_vals[d], 0.0), axis=1, keepdims=True)
    ic = [pick(d, d * 2 * nh + hd) for d, hd in chains]
    fc = [pick(d, d * 2 * nh + nh + hd) for d, hd in chains]
    lf = each(lambda f: jnp.minimum(f, 0.0) - jnp.log(1.0 + jnp.exp(-jnp.abs(f))), fc)
    b1 = []
    for d in range(2):
        cat = jnp.concatenate([jnp.broadcast_to(lf[d * nh + hd], (tc, tc)) for hd in range(nh)], axis=1)
        cum = _dot_exact_lhs(tri_d[d], cat)
        b1 += [cum[:, hd * tc:(hd + 1) * tc] for hd in range(nh)]
    b2 = each(lambda m: m.T, b1)
    ic2 = each(lambda c: jnp.broadcast_to(c, (tc, tc)).T, ic)
    total = each(lambda f: jnp.sum(f, axis=0, keepdims=True), lf)
    m_prev = [m_scr[c, 0:1, 0:1] for c in range(len(chains))]
    logw = [jnp.where(vis_d[d], b1[c] - b2[c] + ic2[c], NEG_INF) for c, (d, _) in enumerate(chains)]
    bcol = each(lambda m: m[:, 0:1], b1)
    inter = each(lambda b, m: b + m, bcol, m_prev)
    m_row = each(lambda lw, it: jnp.maximum(jnp.max(lw, axis=1, keepdims=True), it), logw, inter)
    q = [qk_refs[d][:, hd * dh:(hd + 1) * dh] for d, hd in chains]
    k = [qk_refs[d][:, mw + hd * dh:mw + (hd + 1) * dh] for d, hd in chains]
    v = [v_refs[d][:, hd * dh:(hd + 1) * dh] for d, hd in chains]
    s = each(lambda qq, kk, lw, mr: _dot_nt(qq, kk) * jnp.exp(lw - mr), q, k, logw, m_row)
    w_inter = each(lambda it, mr: jnp.exp(it - mr), inter, m_row)
    c_old = [c_scr[c] for c in range(len(chains))]
    n_old = [n_scr[c, 0:1, :] for c in range(len(chains))]
    num = each(lambda ss, vv, wi, qq, co: _dot(ss.astype(BF16), vv) + wi * _dot(qq, co.astype(BF16)),
               s, v, w_inter, q, c_old)
    den = each(lambda ss, wi, qq, no: jnp.sum(ss, axis=1, keepdims=True)
               + wi * jnp.sum(qq.astype(F32) * no, axis=1, keepdims=True), s, w_inter, q, n_old)
    h = each(lambda nu, de, mr: nu / jnp.maximum(jnp.abs(de), jnp.exp(-mr)), num, den, m_row)
    of_ref[...] = jnp.concatenate(h[:nh], axis=1).astype(of_ref.dtype)
    ob_ref[...] = jnp.concatenate(h[nh:], axis=1).astype(ob_ref.dtype)
    lws = each(lambda t, b, c: t - b + c, total, bcol, ic)
    m_new = each(lambda t, m, l: jnp.maximum(t + m, jnp.max(l, axis=0, keepdims=True)), total, m_prev, lws)
    ek = each(lambda l, m, kk: jnp.exp(l - m) * kk.astype(F32), lws, m_new, k)
    cw = each(lambda t, m, mn: jnp.exp(t + m - mn), total, m_prev, m_new)
    c_new = each(lambda w, co, e, vv: w * co + _dot_tn(e.astype(BF16), vv), cw, c_old, ek, v)
    for c in range(len(chains)):
        c_scr[c] = c_new[c]
        n_scr[c, 0:1, :] = cw[c] * n_old[c] + jnp.sum(ek[c], axis=0, keepdims=True)
        m_scr[c] = jnp.broadcast_to(m_new[c], m_scr.shape[1:])


def mlstm_mix(qk, uvo, gates, *, batch, lc, ctx, nh=M_HEADS, tc=M_CHUNK):
    n = qk.shape[0]
    mw = qk.shape[1] // 2
    dh = mw // nh
    nc = lc // tc
    nc_ctx = ctx // tc
    assert uvo.shape[1] == 3 * mw
    rb = lambda d: (lambda b, j: b * nc + _chunk_order(d, j, nc, nc_ctx))
    ins, args = [], []
    for d in range(2):
        ins += [pl.BlockSpec((tc, 2 * mw), lambda b, j, r=rb(d): (r(b, j), 0)),
                pl.BlockSpec((tc, mw), lambda b, j, r=rb(d): (r(b, j), 1)),
                pl.BlockSpec((tc, 128), lambda b, j, r=rb(d): (r(b, j), 0))]
        args += [qk, uvo, gates]
    return pl.pallas_call(
        functools.partial(_mlstm_kernel, nh=nh),
        out_shape=[jax.ShapeDtypeStruct((n, mw), F32)] * 2,
        grid=(batch, nc),
        in_specs=ins,
        out_specs=[pl.BlockSpec((tc, mw), lambda b, j, r=rb(d): (r(b, j), 0)) for d in range(2)],
        scratch_shapes=[pltpu.VMEM((2 * nh, dh, dh), F32), pltpu.VMEM((2 * nh, 8, dh), F32),
                        pltpu.VMEM((2 * nh, 8, 128), F32)],
        compiler_params=_cparams(("parallel", "arbitrary")),
        name="mlstm",
    )(*args)


def _even_out_kernel(x_ref, g1_ref, y_ref, hf_ref, hb_ref, o_ref, mn_ref, wg_ref, bg_ref, wo_ref,
                     out_ref, mix_scr, *, tm, tpb, ctx, nh):
    i = pl.program_id(0)
    j = pl.program_id(1)

    @pl.when(j == 0)
    def _():
        s = _gelu_tanh(y_ref[...].astype(F32))
        glu = s * _sigmoid(_dot(s.astype(BF16), wg_ref[...]) + bg_ref[...])
        hm = hf_ref[...] + hb_ref[...]
        dh = hm.shape[1] // nh
        parts = []
        for hd in range(nh):
            seg = hm[:, hd * dh:(hd + 1) * dh]
            parts.append(seg * lax.rsqrt(jnp.mean(seg * seg, axis=-1, keepdims=True) + EPS))
        ml = jnp.concatenate(parts, axis=1) * mn_ref[...] * _sigmoid(o_ref[...].astype(F32))
        mix_scr[...] = jnp.concatenate([glu, ml], axis=1).astype(BF16)

    b = i // tpb
    is_ctx = ((i % tpb) * tm + _row_iota(tm)) < ctx
    out_ref[...] = x_ref[...] + _mod_rows(g1_ref, b, is_ctx) * _dot(mix_scr[...], wo_ref[...])


def even_out_block(x, mod, ys5, hf, hb, uvo, ml_norm, w_glu, b_glu, w_out, *, lc, ctx, tm, tn=1024, nh=M_HEADS):
    n, d = x.shape
    sw = ys5.shape[1]
    mw = hf.shape[1]
    assert sw == mw
    nj = d // tn
    return pl.pallas_call(
        functools.partial(_even_out_kernel, tm=tm, tpb=lc // tm, ctx=ctx, nh=nh),
        out_shape=jax.ShapeDtypeStruct((n, d), F32),
        grid=(n // tm, nj),
        in_specs=[pl.BlockSpec((tm, tn), lambda i, j: (i, j)),
                  pl.BlockSpec((8, tn), lambda i, j: (0, 2 * nj + j)),
                  pl.BlockSpec((tm, sw), lambda i, j: (i, 0)),
                  pl.BlockSpec((tm, mw), lambda i, j: (i, 0)),
                  pl.BlockSpec((tm, mw), lambda i, j: (i, 0)),
                  pl.BlockSpec((tm, mw), lambda i, j: (i, 2)),
                  pl.BlockSpec((1, mw), lambda i, j: (0, 0)),
                  pl.BlockSpec((sw, sw), lambda i, j: (0, 0)),
                  pl.BlockSpec((1, sw), lambda i, j: (0, 0)),
                  pl.BlockSpec((sw + mw, tn), lambda i, j: (0, j))],
        out_specs=pl.BlockSpec((tm, tn), lambda i, j: (i, j)),
        scratch_shapes=[pltpu.VMEM((tm, sw + mw), BF16)],
        compiler_params=_cparams(("parallel", "arbitrary")),
        name="even_out",
    )(x, mod, ys5, hf, hb, uvo, ml_norm.reshape(1, mw), w_glu, b_glu.reshape(1, sw), w_out)


def even_layer(x, mod, nw, p, s5_ops, *, batch, lc, ctx, tm):
    qk, uvo, gates = even_in_proj(x, mod, nw, p, lc=lc, ctx=ctx, tm=tm)
    ys5 = s5_mix(uvo, s5_ops, batch=batch, lc=lc, ctx=ctx)
    hf, hb = mlstm_mix(qk, uvo, gates, batch=batch, lc=lc, ctx=ctx)
    return even_out_block(x, mod, ys5, hf, hb, uvo, p["ml_norm"], p["w_glu"], p["b_glu"], p["w_out"],
                          lc=lc, ctx=ctx, tm=tm)


def even_params(w_in, b_in, w_out, w_glu, b_glu, conv_w, conv_b, ml_norm, nh=M_HEADS):
    sw = w_glu.shape[0]
    mw = ml_norm.shape[0]
    c0, c1, c2 = sw, sw + 2 * mw, sw + 4 * mw
    ng = w_in.shape[1] - c2
    scale = jnp.concatenate([jnp.ones((mw,), F32), jnp.full((mw,), (mw // nh) ** -0.5, F32)])
    uvo_cols = lambda a: jnp.concatenate([a[..., :c0], a[..., c1:c2]], axis=-1)
    return dict(
        w_qk=w_in[:, c0:c1].astype(BF16), b_qk=b_in[c0:c1], qk_scale=scale,
        w_uvo=uvo_cols(w_in).astype(BF16), b_uvo=uvo_cols(b_in),
        w_gate=jnp.pad(w_in[:, c2:], ((0, 0), (0, 128 - ng))).astype(BF16),
        b_gate=jnp.pad(b_in[c2:], (0, 128 - ng)),
        conv_w=conv_w, conv_b=conv_b, ml_norm=ml_norm,
        w_glu=w_glu.astype(BF16), b_glu=b_glu, w_out=w_out.astype(BF16))


_LH_W, _LH_A, _LH_G, _LH_V, _LH_END = 0, 256, 512, 768, 896


def _rw_proj_kernel(xm_ref, xp_ref, xn_ref, sh_ref, sc_ref, nw_ref, mu_ref, wl_ref, w_ref,
                    rkv_ref, lh_ref, mix_scr, *, tm, tn, tpb, ctx, lc, nb):
    i = pl.program_id(0)
    j = pl.program_id(1)
    halo = GRID_W

    @pl.when(j == 0)
    def _():
        he, _ = _ext_rows(xm_ref, xp_ref, xn_ref, sh_ref, sc_ref, nw_ref, i, tm, tpb, ctx, halo)
        n = tm + 2 * halo
        d = he.shape[1]
        q = d // 4
        h = he[halo:halo + tm]
        hprev = pltpu.roll(he, 1, 0)[halo:halo + tm]
        hnext = pltpu.roll(he, n - 1, 0)[halo:halo + tm]
        hup = he[0:tm]
        hdown = he[2 * halo:2 * halo + tm]
        pos = (i % tpb) * tm + _row_iota(tm)
        is_ctx = pos < ctx
        pl_ = pos - ctx
        gcol = pl_ & (GRID_W - 1)
        ok_prev = jnp.where(is_ctx, pos, gcol) != 0
        ok_q1 = jnp.where(is_ctx, pos, gcol - (GRID_W - 1)) != 0
        ok_q2 = jnp.where(is_ctx, pos - (ctx - 1), jnp.maximum(pl_ - (GRID_W - 1), 0)) != 0
        ok_q3 = jnp.where(is_ctx, pos - (ctx - 1), jnp.maximum((lc - ctx) - GRID_W - pl_, 0)) != 0
        s0 = jnp.where(ok_prev, hprev[:, :q], 0.0)
        s1 = jnp.where(ok_q1, jnp.where(is_ctx, hprev[:, q:2 * q], hnext[:, q:2 * q]), 0.0)
        s2 = jnp.where(ok_q2, jnp.where(is_ctx, hnext[:, 2 * q:3 * q], hup[:, 2 * q:3 * q]), 0.0)
        s3 = jnp.where(ok_q3, jnp.where(is_ctx, hnext[:, 3 * q:], hdown[:, 3 * q:]), 0.0)
        xx = jnp.concatenate([s0, s1, s2, s3], axis=1) - h
        mix = lambda r: (h + xx * mu_ref[r:r + 1, :]).astype(BF16)
        xv = mix(3)
        mix_scr[0] = mix(0)
        mix_scr[1] = mix(2)
        mix_scr[2] = xv
        lh_ref[:, _LH_W:_LH_A] = jnp.tanh(_dot(mix(1), wl_ref[:, _LH_W:_LH_A]))
        lh_ref[:, _LH_A:_LH_G] = _dot(mix(4), wl_ref[:, _LH_A:_LH_G])
        lh_ref[:, _LH_G:_LH_V] = _sigmoid(_dot(mix(5), wl_ref[:, _LH_G:_LH_V]))
        lh_ref[:, _LH_V:_LH_END] = _dot(xv, wl_ref[:, _LH_V:_LH_END])

    cols = pl.ds(pl.multiple_of((j % nb) * tn, tn), tn)
    rkv_ref[...] = _dot(mix_scr[j // nb], w_ref[j // nb, :, cols])


def rw_project(x, mod, nw, mu, w_lora1, w_rkv, *, lc, ctx, tm=256, tn=2048):
    n, d = x.shape
    nb = d // tn
    return pl.pallas_call(
        functools.partial(_rw_proj_kernel, tm=tm, tn=tn, tpb=lc // tm, ctx=ctx, lc=lc, nb=nb),
        out_shape=[jax.ShapeDtypeStruct((3, n, d), F32), jax.ShapeDtypeStruct((n, _LH_END), F32)],
        grid=(n // tm, 3 * nb),
        in_specs=_halo_specs(tm, d, n, GRID_W) + [
            _mod_spec(d, 0), _mod_spec(d, 1), pl.BlockSpec((1, d), lambda i, j: (0, 0)),
            pl.BlockSpec((6, d), lambda i, j: (0, 0)),
            pl.BlockSpec((d, _LH_END), lambda i, j: (0, 0)),
            pl.BlockSpec((3, d, d), lambda i, j: (0, 0, 0))],
        out_specs=[pl.BlockSpec((None, tm, tn), lambda i, j: (j // nb, i, j % nb)),
                   pl.BlockSpec((tm, _LH_END), lambda i, j: (i, 0))],
        scratch_shapes=[pltpu.VMEM((3, tm, d), BF16)],
        compiler_params=_cparams(("parallel", "arbitrary")),
        name="rw_project",
    )(x, x, x, mod, mod, nw.reshape(1, d), mu, w_lora1, w_rkv)


def _head_sums(x, bd):
    w = bd.shape[0]
    hi = x.astype(BF16)
    lo = (x - hi.astype(F32)).astype(BF16)
    parts = []
    for c in range(x.shape[1] // w):
        sl = slice(c * w, (c + 1) * w)
        both = _dot(jnp.concatenate([hi[:, sl], lo[:, sl]], axis=0), bd)
        parts.append(both[:x.shape[0]] + both[x.shape[0]:])
    return jnp.concatenate(parts, axis=1)


def _head_bd():
    r = lax.broadcasted_iota(jnp.int32, (_MXU_N, _MXU_N), 0) // R_HEAD
    c = lax.broadcasted_iota(jnp.int32, (_MXU_N, _MXU_N), 1) // R_HEAD
    return jnp.where(r == c, 1.0, 0.0).astype(BF16)


def _rw_gate_kernel(*refs, tm, has_vfirst):
    if has_vfirst:
        (rkv_ref, lh_ref, vf_ref, w2_ref, a2_ref, g2_ref, v2_ref, pv_ref,
         t6_ref, gt_ref, v_ref, aux_ref) = refs
    else:
        rkv_ref, lh_ref, w2_ref, a2_ref, g2_ref, pv_ref, t6_ref, gt_ref, v_ref, aux_ref = refs
    r = rkv_ref[0]
    k = rkv_ref[1]
    v = rkv_ref[2]
    pv = pv_ref[...]
    seg = lambda a, b: lh_ref[:, a:b].astype(BF16)
    if has_vfirst:
        v = v + (vf_ref[...] - v) * _sigmoid(pv[4:5] + _dot(seg(_LH_V, _LH_END), v2_ref[...]))
    v_ref[...] = v
    bd = _head_bd()
    kk = k * pv[5:6]
    kk = kk * lax.rsqrt(jnp.maximum(_head_sums(kk * kk, bd), 1e-24))
    aux_ref[1] = _dot(seg(_LH_G, _LH_V), g2_ref[...]).astype(aux_ref.dtype)
    row = lax.broadcasted_iota(jnp.int32, (tm, tm), 0)
    col = lax.broadcasted_iota(jnp.int32, (tm, tm), 1)
    same = (row // R_CHUNK) == (col // R_CHUNK)
    hw = seg(_LH_W, _LH_A)
    ha = seg(_LH_A, _LH_G)
    ksum = jnp.zeros_like(k)
    nchunk = tm // R_CHUNK
    for d in range(2):
        lw = _sigmoid(pv[d:d + 1] + _dot(hw, w2_ref[d])) * (-_EXP_M05)
        a = _sigmoid(pv[2 + d:3 + d] + _dot(ha, a2_ref[d]))
        kd = k * (1.0 + (a - 1.0) * pv[6:7])
        bv = kk * a
        ksum = ksum + kd
        tri = jnp.where(same & ((col <= row) if d == 0 else (col >= row)), 1.0, 0.0).astype(BF16)
        cum = _dot_01_lhs(tri, lw)
        end = R_CHUNK - 1 if d == 0 else 0
        gt = jnp.exp(cum.reshape(nchunk, R_CHUNK, cum.shape[1])[:, end:end + 1, :])
        e_pos = jnp.exp(cum)
        e_neg = jnp.exp(-cum)
        e_end = (e_neg.reshape(nchunk, R_CHUNK, cum.shape[1]) * gt).reshape(cum.shape)
        t6_ref[d, 0] = (r * e_pos).astype(BF16)
        t6_ref[d, 1] = (-kk * jnp.exp(cum - lw)).astype(BF16)
        t6_ref[d, 2] = (kd * e_neg).astype(BF16)
        t6_ref[d, 3] = (bv * e_neg).astype(BF16)
        t6_ref[d, 4] = (kd * e_end).astype(BF16)
        t6_ref[d, 5] = (bv * e_end).astype(BF16)
        for c in range(nchunk):
            gt_ref[d, c] = gt[c]
    bonus = _head_sums(r * ksum * pv[7:8], bd) * v
    aux_ref[0] = bonus.astype(aux_ref.dtype)


def rw_gates(rkv, lh, v_first, w2, a2, g2, v2, pvec, *, tm=256, tc=1024):
    _, n, d = rkv.shape
    has_vf = v_first is not None
    tile = pl.BlockSpec((tm, tc), lambda i, j: (i, j))
    in_specs = [pl.BlockSpec((3, tm, tc), lambda i, j: (0, i, j)),
                pl.BlockSpec((tm, _LH_END), lambda i, j: (i, 0))]
    args = [rkv, lh]
    if has_vf:
        in_specs.append(tile)
        args.append(v_first)
    in_specs += [pl.BlockSpec((2, 256, tc), lambda i, j: (0, 0, j)),
                 pl.BlockSpec((2, 256, tc), lambda i, j: (0, 0, j)),
                 pl.BlockSpec((256, tc), lambda i, j: (0, j))]
    args += [w2, a2, g2]
    if has_vf:
        in_specs.append(pl.BlockSpec((128, tc), lambda i, j: (0, j)))
        args.append(v2)
    in_specs.append(pl.BlockSpec((8, tc), lambda i, j: (0, j)))
    args.append(pvec)
    nch = tm // R_CHUNK
    return pl.pallas_call(
        functools.partial(_rw_gate_kernel, tm=tm, has_vfirst=has_vf),
        out_shape=[jax.ShapeDtypeStruct((2, 6, n, d), BF16),
                   jax.ShapeDtypeStruct((2, n // R_CHUNK, 1, d), F32),
                   jax.ShapeDtypeStruct((n, d), F32),
                   jax.ShapeDtypeStruct((2, n, d), BF16)],
        grid=(n // tm, d // tc),
        in_specs=in_specs,
        out_specs=[pl.BlockSpec((2, 6, tm, tc), lambda i, j: (0, 0, i, j)),
                   pl.BlockSpec((2, nch, 1, tc), lambda i, j: (0, i, 0, j)),
                   tile,
                   pl.BlockSpec((2, tm, tc), lambda i, j: (0, i, j))],
        compiler_params=_cparams(("parallel", "parallel")),
        name="rw_gates",
    )(*args)


def _pair_stack(y, hi_lane):
    z = jnp.zeros_like(y)
    return jnp.concatenate([jnp.where(hi_lane, z, y), jnp.where(hi_lane, y, z)], axis=0)


def _rw_chunk_kernel(t6_ref, v_ref, gt_ref, y_ref, s_scr, *, npair, nsub):
    d = pl.program_id(1)
    tc = R_CHUNK

    @pl.when(pl.program_id(2) == 0)
    def _():
        s_scr[...] = jnp.zeros_like(s_scr)

    lane = lax.broadcasted_iota(jnp.int32, (tc, 128), 1)
    row = lax.broadcasted_iota(jnp.int32, (tc, 128), 0)
    hi_lane = lane >= R_HEAD
    rel = ((lane & (R_HEAD - 1)) - row) * (1 - 2 * d)
    strict = rel < 0
    incl = rel <= 0
    eye2 = jnp.where(rel == 0, 1.0, 0.0)

    stack = lambda y: _pair_stack(y.astype(BF16), hi_lane)
    rows = lambda a, b: jnp.concatenate([a.astype(BF16), b.astype(BF16)], axis=0)
    top, bot = (lambda m: m[:tc]), (lambda m: m[tc:])
    lft, rgt = (lambda m: m[:, :128]), (lambda m: m[:, 128:])

    def pm(a, y):
        return _dot(a.astype(BF16), stack(y))

    def pm2(a, y1, y2):
        return _dot(a.astype(BF16), jnp.concatenate([stack(y1), stack(y2)], axis=1))

    def pack_kv(full):
        return jnp.where(hi_lane, full[R_HEAD:], full[:R_HEAD])

    def chunk(ci):
        rs = pl.ds(pl.multiple_of(ci * tc, tc), tc)
        sls = [slice(u * 128, (u + 1) * 128) for u in range(npair)]
        each = lambda f, *cols: [f(*xs) for xs in zip(*cols)]
        rt, at, kt, bt, kh, bh = ([t6_ref[c, rs, sl] for sl in sls] for c in range(6))
        v = [v_ref[rs, sl].astype(BF16) for sl in sls]
        ar = each(rows, at, rt)
        xb = each(lambda x, y: _dot_nt(x, stack(y)), ar, bt)
        xk = each(lambda x, y: _dot_nt(x, stack(y)), ar, kt)
        lab = each(lambda m: jnp.where(strict, top(m), 0.0), xb)
        arb = each(lambda m: jnp.where(incl, bot(m), 0.0), xb)
        lak = each(lambda m: jnp.where(strict, top(m), 0.0), xk)
        ark = each(lambda m: jnp.where(incl, bot(m), 0.0), xk)
        inv = each(lambda l: eye2 + l, lab)
        pw = each(pm, lab, lab)
        for _ in range(4):
            both = each(lambda i, p: pm(rows(i, p), p), inv, pw)
            inv = each(lambda i, m: i + top(m), inv, both)
            pw = each(bot, both)
        inv = each(lambda i, p: i + pm(i, p), inv, pw)
        lv = each(lambda a, b, vv: pm(rows(a, b), vv), lak, ark, v)
        m2 = each(top, lv)
        wu = each(pm2, inv, at, m2)
        au = each(lambda a, m: pm2(a, lft(m), rgt(m)), arb, wu)
        q = each(lambda r, m: r.astype(F32) + lft(m), rt, au)
        y0 = each(lambda m, n: bot(m) + rgt(n), lv, au)
        bwu = each(lambda b, m: _dot_tn(b, m.astype(BF16)), bh, wu)
        g = each(lambda sl, m: jnp.where(rel == 0, gt_ref[ci, :, sl], 0.0) + pack_kv(lft(m)), sls, bwu)
        h = each(lambda k, vv, m: pack_kv(_dot_tn(k, vv)) + pack_kv(rgt(m)), kh, v, bwu)
        s_old = [s_scr[:, sl] for sl in sls]
        s_hi = each(lambda s: s.astype(BF16), s_old)
        s_lo = each(lambda s, hi: (s - hi.astype(F32)).astype(BF16), s_old, s_hi)
        gq = each(rows, g, q)
        res = each(lambda m, hi, lo: _dot(m, _pair_stack(hi, hi_lane)) + _dot(m, _pair_stack(lo, hi_lane)),
                   gq, s_hi, s_lo)
        for u, sl in enumerate(sls):
            s_scr[:, sl] = top(res[u]) + h[u]
            y_ref[rs, sl] = bot(res[u]) + y0[u]

    for s in range(nsub):
        chunk(s + d * (nsub - 1 - 2 * s))


def rw_chunk_scan(t6, v, gt, *, batch, lc, ctx, nsub=2):
    _, _, n, d = t6.shape
    tc = R_CHUNK
    nb = lc // (tc * nsub)
    nb_ctx = ctx // (tc * nsub)
    assert lc % (tc * nsub) == 0 and ctx % (tc * nsub) == 0
    rb = lambda b, e, j: b * nb + _chunk_order(e, j, nb, nb_ctx)
    return pl.pallas_call(
        functools.partial(_rw_chunk_kernel, npair=d // 128, nsub=nsub),
        out_shape=jax.ShapeDtypeStruct((2, n, d), F32),
        grid=(batch, 2, nb),
        in_specs=[pl.BlockSpec((None, 6, nsub * tc, d), lambda b, e, j: (e, 0, rb(b, e, j), 0)),
                  pl.BlockSpec((nsub * tc, d), lambda b, e, j: (rb(b, e, j), 0)),
                  pl.BlockSpec((None, nsub, 1, d), lambda b, e, j: (e, rb(b, e, j), 0, 0))],
        out_specs=pl.BlockSpec((None, nsub * tc, d), lambda b, e, j: (e, rb(b, e, j), 0)),
        scratch_shapes=[pltpu.VMEM((tc, d), F32)],
        compiler_params=_cparams(("parallel", "parallel", "arbitrary")),
        name="rw_chunk_scan",
    )(t6, v, gt)


def _rw_out_kernel(x_ref, g1_ref, y_ref, aux_ref, ln_ref, wo_ref, out_ref, z_scr, *, tm, tpb, ctx):
    i = pl.program_id(0)
    j = pl.program_id(1)

    @pl.when(j == 0)
    def _():
        bd = _head_bd()
        for c in range(y_ref.shape[2] // _MXU_N):
            sl = slice(c * _MXU_N, (c + 1) * _MXU_N)
            y = y_ref[0, :, sl] + y_ref[1, :, sl]
            mu = _head_sums(y, bd) * (1.0 / R_HEAD)
            yc = y - mu
            var = _head_sums(yc * yc, bd) * (1.0 / R_HEAD)
            z = yc * lax.rsqrt(var + R_LN_EPS) * ln_ref[0:1, sl] + ln_ref[1:2, sl]
            z_scr[:, sl] = ((z + aux_ref[0, :, sl].astype(F32)) * aux_ref[1, :, sl].astype(F32)).astype(BF16)

    b = i // tpb
    is_ctx = ((i % tpb) * tm + _row_iota(tm)) < ctx
    out_ref[...] = x_ref[...] + _mod_rows(g1_ref, b, is_ctx) * _dot(z_scr[...], wo_ref[...])


def rw_out_block(x, mod, y, aux, ln_wb, w_o, *, lc, ctx, tm, tn=1024):
    n, d = x.shape
    nj = d // tn
    return pl.pallas_call(
        functools.partial(_rw_out_kernel, tm=tm, tpb=lc // tm, ctx=ctx),
        out_shape=jax.ShapeDtypeStruct((n, d), F32),
        grid=(n // tm, nj),
        in_specs=[pl.BlockSpec((tm, tn), lambda i, j: (i, j)),
                  pl.BlockSpec((8, tn), lambda i, j: (0, 2 * nj + j)),
                  pl.BlockSpec((2, tm, d), lambda i, j: (0, i, 0)),
                  pl.BlockSpec((2, tm, d), lambda i, j: (0, i, 0)),
                  pl.BlockSpec((2, d), lambda i, j: (0, 0)),
                  pl.BlockSpec((d, tn), lambda i, j: (0, j))],
        out_specs=pl.BlockSpec((tm, tn), lambda i, j: (i, j)),
        scratch_shapes=[pltpu.VMEM((tm, d), BF16)],
        compiler_params=_cparams(("parallel", "arbitrary")),
        name="rw_out",
    )(x, mod, y, aux, ln_wb, w_o)


def odd_params(mu, w_r, w_k, w_v, w_o, w0, w1, w2, a0, a1, a2, g1, g2, k_k, k_a, r_k, ln_w, ln_b,
               v0=None, v1=None, v2=None):
    d = w_r.shape[0]
    dw, da, dg = w1.shape[-1], a1.shape[-1], g1.shape[-1]
    assert 2 * dw <= _LH_A - _LH_W and 2 * da <= _LH_G - _LH_A and dg <= _LH_V - _LH_G
    wl = jnp.zeros((d, _LH_END), F32)
    wl = wl.at[:, _LH_W:_LH_W + dw].set(w1[0]).at[:, _LH_W + dw:_LH_W + 2 * dw].set(w1[1])
    wl = wl.at[:, _LH_A:_LH_A + da].set(a1[0]).at[:, _LH_A + da:_LH_A + 2 * da].set(a1[1])
    wl = wl.at[:, _LH_G:_LH_G + dg].set(g1)
    w2p = jnp.zeros((2, 256, d), F32).at[0, :dw].set(w2[0]).at[1, dw:2 * dw].set(w2[1])
    a2p = jnp.zeros((2, 256, d), F32).at[0, :da].set(a2[0]).at[1, da:2 * da].set(a2[1])
    g2p = jnp.zeros((256, d), F32).at[:dg].set(g2)
    v2p = None
    vzero = jnp.zeros((d,), F32)
    if v1 is not None:
        dv = v1.shape[-1]
        assert dv <= _LH_END - _LH_V
        wl = wl.at[:, _LH_V:_LH_V + dv].set(v1)
        v2p = jnp.zeros((128, d), F32).at[:dv].set(v2).astype(BF16)
    pvec = jnp.stack([w0[0], w0[1], a0[0], a0[1], v0 if v0 is not None else vzero,
                      k_k, k_a, r_k.reshape(d)])
    return dict(mu=mu, w_lora1=wl.astype(BF16), w_rkv=jnp.stack([w_r, w_k, w_v]).astype(BF16),
                w2=w2p.astype(BF16), a2=a2p.astype(BF16), g2=g2p.astype(BF16), v2=v2p, pvec=pvec,
                ln_wb=jnp.stack([ln_w, ln_b]), w_o=w_o.astype(BF16))


def odd_layer(x, mod, nw, p, v_first, *, batch, lc, ctx, tm):
    rkv, lh = rw_project(x, mod, nw, p["mu"], p["w_lora1"], p["w_rkv"], lc=lc, ctx=ctx)
    t6, gt, v, aux = rw_gates(rkv, lh, v_first if p["v2"] is not None else None,
                              p["w2"], p["a2"], p["g2"], p["v2"], p["pvec"])
    y = rw_chunk_scan(t6, v, gt, batch=batch, lc=lc, ctx=ctx)
    return rw_out_block(x, mod, y, aux, p["ln_wb"], p["w_o"], lc=lc, ctx=ctx, tm=tm), v


def _row_tile(lc, cap):
    return max(t for t in range(16, cap + 1, 16) if lc % t == 0)


def kernel(x, c, ctx, c_ctx, ada_w, ada_b, norm_mix, norm_ffn, ffn_w_up, ffn_conv_w, ffn_conv_b, ffn_w_down, norm_final, ev_w_in, ev_b_in, ev_w_out, s5_lam_re, s5_lam_im, s5_log_step, s5_b_re, s5_b_im, s5_c_re, s5_c_im, s5_d, s5_w_glu, s5_b_glu, ml_conv_w, ml_conv_b, ml_norm, rw_mu, rw_w_r, rw_w_k, rw_w_v, rw_w_o, rw_w0, rw_w1, rw_w2, rw_a0, rw_a1, rw_a2, rw_v0, rw_v1, rw_v2, rw_g1, rw_g2, rw_k_k, rw_k_a, rw_r_k, rw_ln_w, rw_ln_b):
    batch, seq, d = x.shape
    n_ctx = ctx.shape[1]
    lc = n_ctx + seq
    depth = ada_w.shape[0]
    n_even = ev_w_in.shape[0]
    assert batch < _CTX_ROW and n_ctx % 256 == 0 and seq % 256 == 0 and seq % GRID_W == 0
    tm = _row_tile(lc, 544)
    tm_small = _row_tile(lc, 272)

    xc = jnp.concatenate([ctx, x], axis=1).reshape(batch * lc, d)
    c8 = jnp.zeros((8, d), F32).at[:batch].set(c).at[_CTX_ROW].set(c_ctx)
    mods = adaln(c8, ada_w, ada_b)

    g, p_state = s5_lam_re.shape[-2:]
    flat = lambda a: a.reshape((n_even * 2,) + a.shape[2:])
    s5p = s5_params(flat(s5_lam_re), flat(s5_lam_im), s5_log_step.reshape(n_even * 2, g),
                    flat(s5_b_re), flat(s5_b_im), flat(s5_c_re), flat(s5_c_im))

    v_first = None
    for l in range(depth):
        j = l // 2
        if l % 2 == 0:
            ep = even_params(ev_w_in[j], ev_b_in[j], ev_w_out[j], s5_w_glu[j], s5_b_glu[j],
                             ml_conv_w[j], ml_conv_b[j], ml_norm[j])
            ops = s5_assemble(tuple(a[2 * j:2 * j + 2] for a in s5p), s5_d[j])
            xc = even_layer(xc, mods[l], norm_mix[l], ep, ops, batch=batch, lc=lc, ctx=n_ctx, tm=tm)
        else:
            extra = {} if j == 0 else dict(v0=rw_v0[j - 1], v1=rw_v1[j - 1], v2=rw_v2[j - 1])
            op = odd_params(rw_mu[j], rw_w_r[j], rw_w_k[j], rw_w_v[j], rw_w_o[j], rw_w0[j], rw_w1[j],
                            rw_w2[j], rw_a0[j], rw_a1[j], rw_a2[j], rw_g1[j], rw_g2[j], rw_k_k[j],
                            rw_k_a[j], rw_r_k[j], rw_ln_w[j], rw_ln_b[j], **extra)
            xc, v = odd_layer(xc, mods[l], norm_mix[l], op, v_first, batch=batch, lc=lc, ctx=n_ctx,
                              tm=tm_small)
            if j == 0:
                v_first = v
        xc = conv_ffn_block(xc, mods[l], norm_ffn[l], ffn_w_up[l].astype(BF16), ffn_conv_w[l],
                            ffn_conv_b[l], ffn_w_down[l].astype(BF16), lc=lc, ctx=n_ctx, tm=tm)
    out = final_norm(xc, norm_final, batch=batch, lc=lc, ctx=n_ctx)
    return out.reshape(batch, seq, d)
```
